```python
import math
import jax, jax.numpy as jnp
from jax import lax
import numpy as np

D_MODEL = 1024
BATCH = 8
SEQ = 4096
DEPTH = 4

N_MIXERS = 2
N_SSD_LAYERS = (DEPTH + N_MIXERS - 1) // N_MIXERS
N_LRU_LAYERS = DEPTH // N_MIXERS
SSD_EXPAND = 2
SSD_D_INNER = SSD_EXPAND * D_MODEL
SSD_HEADDIM = 64
SSD_HEADS = SSD_D_INNER // SSD_HEADDIM
SSD_GROUPS = 4
SSD_HPG = SSD_HEADS // SSD_GROUPS
SSD_D_STATE = 128
SSD_CONV = 4
SSD_CHUNK = 128
SSD_CONV_DIM = SSD_D_INNER + 2 * SSD_GROUPS * SSD_D_STATE
SSD_IN_DIM = SSD_D_INNER + SSD_CONV_DIM + SSD_HEADS
LRU_WIDTH = 1280
LRU_HEADS = 10
LRU_BLOCK = LRU_WIDTH // LRU_HEADS
LRU_CONV = 4
LRU_C = 8.0
D_FF = 4 * D_MODEL
FFN_CONV = 3
EPS = 1e-6

kernel_name = 'hybrid_ssd_rglru_convffn_trunk'


def rmsnorm(x, w):
    xf = x.astype(jnp.float32)
    y = xf * lax.rsqrt(jnp.mean(xf * xf, axis=-1, keepdims=True) + EPS)
    return (y * w.astype(jnp.float32)).astype(x.dtype)


def causal_dwconv(x, w, b):
    k = w.shape[0]
    s = x.shape[1]
    xp = jnp.pad(x, ((0, 0), (k - 1, 0), (0, 0)))
    y = b
    for j in range(k):
        y = y + xp[:, j:j + s, :] * w[j]
    return y


def ssd_chunked_scan(xh, dt, a, bm, cm):
    bsz, s, h, p = xh.shape
    nc = s // SSD_CHUNK
    xdt = (xh * dt[..., None]).reshape(bsz, nc, SSD_CHUNK, SSD_GROUPS, SSD_HPG, p)
    da = (dt * a).reshape(bsz, nc, SSD_CHUNK, SSD_GROUPS, SSD_HPG)
    bc = bm.reshape(bsz, nc, SSD_CHUNK, SSD_GROUPS, SSD_D_STATE)
    cc = cm.reshape(bsz, nc, SSD_CHUNK, SSD_GROUPS, SSD_D_STATE)
    cs = jnp.moveaxis(jnp.cumsum(da, axis=2), 2, -1)
    tri = jnp.tril(jnp.ones((SSD_CHUNK, SSD_CHUNK), dtype=bool))
    decay_in = jnp.exp(jnp.where(tri, cs[..., :, None] - cs[..., None, :], -jnp.inf))
    cb = jnp.einsum('bclgn,bcsgn->bcgls', cc, bc)
    scores = cb[:, :, :, None] * decay_in
    y_diag = jnp.einsum('bcgels,bcsgep->bclgep', scores, xdt)
    decay_out = jnp.moveaxis(jnp.exp(cs[..., -1:] - cs), -1, 2)
    states = jnp.einsum('bclgn,bclgep->bcgepn', bc, xdt * decay_out[..., None])
    tot = cs[..., -1]
    inc = jnp.moveaxis(jnp.cumsum(tot, axis=1), 1, -1)
    exc = inc - jnp.moveaxis(tot, 1, -1)
    stri = jnp.tril(jnp.ones((nc, nc), dtype=bool), -1)
    decay_chunk = jnp.exp(jnp.where(stri, exc[..., :, None] - inc[..., None, :], -jnp.inf))
    entering = jnp.einsum('bgezc,bcgepn->bzgepn', decay_chunk, states)
    state_decay = jnp.moveaxis(jnp.exp(cs), -1, 2)
    y_off = jnp.einsum('bclgn,bcgepn->bclgep', cc, entering) * state_decay[..., None]
    return (y_diag + y_off).reshape(bsz, s, h, p)


def gated_group_rmsnorm(y, z, w):
    bsz, s, d = y.shape
    g = (y * jax.nn.silu(z.astype(jnp.float32))).reshape(bsz, s, SSD_GROUPS, d // SSD_GROUPS)
    g = g * lax.rsqrt(jnp.mean(g * g, axis=-1, keepdims=True) + EPS)
    return g.reshape(bsz, s, d) * w.astype(jnp.float32)


def ssd_mixer(u, w_in, conv_w, conv_b, dt_bias, a_log, d_skip, norm_w, w_out):
    bsz, s, _ = u.shape
    proj = u @ w_in
    z = proj[..., :SSD_D_INNER]
    xbc = proj[..., SSD_D_INNER:SSD_D_INNER + SSD_CONV_DIM]
    dt_raw = proj[..., SSD_D_INNER + SSD_CONV_DIM:]
    xbc = jax.nn.silu(causal_dwconv(xbc, conv_w, conv_b)).astype(jnp.float32)
    xs = xbc[..., :SSD_D_INNER].reshape(bsz, s, SSD_HEADS, SSD_HEADDIM)
    bm = xbc[..., SSD_D_INNER:SSD_D_INNER + SSD_GROUPS * SSD_D_STATE].reshape(bsz, s, SSD_GROUPS, SSD_D_STATE)
    cm = xbc[..., SSD_D_INNER + SSD_GROUPS * SSD_D_STATE:].reshape(bsz, s, SSD_GROUPS, SSD_D_STATE)
    dt = jax.nn.softplus(dt_raw.astype(jnp.float32) + dt_bias.astype(jnp.float32))
    a = -jnp.exp(a_log.astype(jnp.float32))
    y = ssd_chunked_scan(xs, dt, a, bm, cm)
    y = y + d_skip.astype(jnp.float32)[:, None] * xs
    y = gated_group_rmsnorm(y.reshape(bsz, s, SSD_D_INNER), z, norm_w)
    return (y.astype(u.dtype) @ w_out).astype(u.dtype)


def _linear_combine(c1, c2):
    a1, b1 = c1
    a2, b2 = c2
    return a1 * a2, a2 * b1 + b2


def rglru_mixer(u, w_in, b_in, conv_w, conv_b, w_gx, b_gx, w_ga, b_ga, lam, w_out, b_out):
    bsz, s, _ = u.shape
    proj = u @ w_in + b_in
    ybr = jax.nn.gelu(proj[..., :LRU_WIDTH].astype(jnp.float32))
    xbr = causal_dwconv(proj[..., LRU_WIDTH:], conv_w, conv_b)
    xh = xbr.reshape(bsz, s, LRU_HEADS, LRU_BLOCK)
    gx = jax.nn.sigmoid((jnp.einsum('bshi,hij->bshj', xh, w_gx).reshape(bsz, s, LRU_WIDTH) + b_gx).astype(jnp.float32))
    ga = jax.nn.sigmoid((jnp.einsum('bshi,hij->bshj', xh, w_ga).reshape(bsz, s, LRU_WIDTH) + b_ga).astype(jnp.float32))
    log_a = LRU_C * ga * jax.nn.log_sigmoid(lam.astype(jnp.float32))
    a_t = jnp.exp(log_a)
    mult = jnp.sqrt(-jnp.expm1(2.0 * log_a))
    b_t = mult * gx * xbr.astype(jnp.float32)
    _, h = lax.associative_scan(_linear_combine, (a_t, b_t), axis=1)
    out = (h * ybr).astype(u.dtype) @ w_out + b_out
    return out.astype(u.dtype)


def conv_ffn(u, w_up, conv_w, conv_b, w_down):
    hcv = causal_dwconv(u @ w_up, conv_w, conv_b)
    g = hcv[..., :D_FF]
    v = hcv[..., D_FF:]
    return ((jax.nn.gelu(g.astype(jnp.float32)) * v.astype(jnp.float32)).astype(u.dtype) @ w_down).astype(u.dtype)


def setup_inputs(seed: int = 0) -> dict:
    key = jax.random.key(seed)
    ks = jax.random.split(key, 32)
    f32 = jnp.float32

    def nrm(k, shape, scale):
        return scale * jax.random.normal(k, shape, f32)

    def gain(k, shape):
        return 1.0 + 0.1 * jax.random.normal(k, shape, f32)

    na, nb = N_SSD_LAYERS, N_LRU_LAYERS
    dt0 = jnp.exp(jax.random.uniform(ks[9], (na, SSD_HEADS), f32, math.log(1e-3), math.log(1e-1)))
    lam_u = jax.random.uniform(ks[20], (nb, LRU_WIDTH), f32, 0.9, 0.999)
    return {
        'x': jax.random.normal(ks[0], (BATCH, SEQ, D_MODEL), f32),
        'norm_mix_pre': gain(ks[1], (DEPTH, D_MODEL)),
        'norm_mix_post': gain(ks[2], (DEPTH, D_MODEL)),
        'norm_ffn_pre': gain(ks[3], (DEPTH, D_MODEL)),
        'norm_ffn_post': gain(ks[4], (DEPTH, D_MODEL)),
        'ssd_w_in': nrm(ks[5], (na, D_MODEL, SSD_IN_DIM), D_MODEL ** -0.5),
        'ssd_conv_w': nrm(ks[6], (na, SSD_CONV, SSD_CONV_DIM), SSD_CONV ** -0.5),
        'ssd_conv_b': nrm(ks[7], (na, SSD_CONV_DIM), 0.01),
        'ssd_dt_bias': dt0 + jnp.log(-jnp.expm1(-dt0)),
        'ssd_a_log': jnp.log(jax.random.uniform(ks[10], (na, SSD_HEADS), f32, 1.0, 16.0)),
        'ssd_d': gain(ks[11], (na, SSD_HEADS)),
        'ssd_norm': gain(ks[12], (na, SSD_D_INNER)),
        'ssd_w_out': nrm(ks[13], (na, SSD_D_INNER, D_MODEL), SSD_D_INNER ** -0.5),
        'lru_w_in': nrm(ks[14], (nb, D_MODEL, 2 * LRU_WIDTH), D_MODEL ** -0.5),
        'lru_b_in': nrm(ks[15], (nb, 2 * LRU_WIDTH), 0.01),
        'lru_conv_w': nrm(ks[16], (nb, LRU_CONV, LRU_WIDTH), LRU_CONV ** -0.5),
        'lru_conv_b': nrm(ks[17], (nb, LRU_WIDTH), 0.01),
        'lru_w_gx': nrm(ks[18], (nb, LRU_HEADS, LRU_BLOCK, LRU_BLOCK), LRU_BLOCK ** -0.5),
        'lru_b_gx': nrm(ks[19], (nb, LRU_WIDTH), 0.01),
        'lru_w_ga': nrm(ks[21], (nb, LRU_HEADS, LRU_BLOCK, LRU_BLOCK), LRU_BLOCK ** -0.5),
        'lru_b_ga': nrm(ks[22], (nb, LRU_WIDTH), 0.01),
        'lru_lambda': jnp.log(lam_u) - jnp.log1p(-lam_u),
        'lru_w_out': nrm(ks[23], (nb, LRU_WIDTH, D_MODEL), LRU_WIDTH ** -0.5),
        'lru_b_out': nrm(ks[24], (nb, D_MODEL), 0.01),
        'ffn_w_up': nrm(ks[25], (DEPTH, D_MODEL, 2 * D_FF), D_MODEL ** -0.5),
        'ffn_conv_w': nrm(ks[26], (DEPTH, FFN_CONV, 2 * D_FF), FFN_CONV ** -0.5),
        'ffn_conv_b': nrm(ks[27], (DEPTH, 2 * D_FF), 0.01),
        'ffn_w_down': nrm(ks[28], (DEPTH, D_FF, D_MODEL), D_FF ** -0.5),
    }


def reference(x, norm_mix_pre, norm_mix_post, norm_ffn_pre, norm_ffn_post,
              ssd_w_in, ssd_conv_w, ssd_conv_b, ssd_dt_bias, ssd_a_log, ssd_d, ssd_norm, ssd_w_out,
              lru_w_in, lru_b_in, lru_conv_w, lru_conv_b, lru_w_gx, lru_b_gx, lru_w_ga, lru_b_ga,
              lru_lambda, lru_w_out, lru_b_out,
              ffn_w_up, ffn_conv_w, ffn_conv_b, ffn_w_down):
    for i in range(DEPTH):
        h = rmsnorm(x, norm_mix_pre[i])
        j = i // N_MIXERS
        if i % N_MIXERS == 0:
            m = ssd_mixer(h, ssd_w_in[j], ssd_conv_w[j], ssd_conv_b[j], ssd_dt_bias[j],
                          ssd_a_log[j], ssd_d[j], ssd_norm[j], ssd_w_out[j])
        else:
            m = rglru_mixer(h, lru_w_in[j], lru_b_in[j], lru_conv_w[j], lru_conv_b[j],
                            lru_w_gx[j], lru_b_gx[j], lru_w_ga[j], lru_b_ga[j],
                            lru_lambda[j], lru_w_out[j], lru_b_out[j])
        x = x + rmsnorm(m, norm_mix_post[i])
        h = rmsnorm(x, norm_ffn_pre[i])
        f = conv_ffn(h, ffn_w_up[i], ffn_conv_w[i], ffn_conv_b[i], ffn_w_down[i])
        x = x + rmsnorm(f, norm_ffn_post[i])
    return x
```

```python
import functools
import math

import jax
import jax.numpy as jnp
from jax import lax
from jax.experimental import pallas as pl
from jax.experimental.pallas import tpu as pltpu

F32 = jnp.float32
BF16 = jnp.bfloat16
EPS = 1e-6

SUBLANES = 8
LANES = 128
VMEM_LIMIT_BYTES = 56 * 1024 * 1024

SSD_HEADDIM = 64
SSD_GROUPS = 4
SSD_D_STATE = 128
SSD_CHUNK = 128
LRU_HEADS = 10
LRU_BLOCK = 128
LRU_C = 8.0

FFN_TM = 512
FFN_FC = 512
LRU_TL = 256


def _rmsnorm(x, w):
    ms = jnp.mean(x * x, axis=-1, keepdims=True)
    return x * lax.rsqrt(ms + EPS) * w


def _gelu_tanh(x):
    c = math.sqrt(2.0 / math.pi)
    return x * (0.5 * (1.0 + jnp.tanh(c * (x + 0.044715 * (x * x * x)))))


def _sigmoid(x):
    return 1.0 / (1.0 + jnp.exp(-x))


def _softplus(x):
    return jnp.maximum(x, 0.0) + jnp.log1p(jnp.exp(-jnp.abs(x)))


def _shift_rows(u, prev, k):
    rows = lax.broadcasted_iota(jnp.int32, (SUBLANES, u.shape[1]), 0)
    rolled = pltpu.roll(u, k, 0)
    head = jnp.where(rows < k, pltpu.roll(prev, k, 0), rolled[:SUBLANES])
    return jnp.concatenate([head, rolled[SUBLANES:]], axis=0)


def _causal_conv(u, prev, w, b):
    k = w.shape[0]
    y = b
    for j in range(k):
        shift = k - 1 - j
        tap = u if shift == 0 else _shift_rows(u, prev, shift)
        y = y + tap * w[j:j + 1, :]
    return y


def _ffn_kernel(x_ref, wpre_ref, wg_ref, wv_ref, cwg_ref, cwv_ref, cbg_ref, cbv_ref, wd_ref, wpost_ref,
                o_ref, h_sc, acc_sc, cg_sc, cv_sc):
    t = pl.program_id(1)
    f = pl.program_id(2)
    nf = pl.num_programs(2)
    tm = h_sc.shape[0]

    @pl.when(f == 0)
    def _():
        h_sc[...] = _rmsnorm(x_ref[0], wpre_ref[...]).astype(BF16)

    @pl.when(t == 0)
    def _():
        cg_sc[f] = jnp.zeros(cg_sc.shape[1:], F32)
        cv_sc[f] = jnp.zeros(cv_sc.shape[1:], F32)

    h = h_sc[...]
    ug = jnp.dot(h, wg_ref[...], preferred_element_type=F32)
    uv = jnp.dot(h, wv_ref[...], preferred_element_type=F32)
    pg = cg_sc[f]
    pv = cv_sc[f]
    cg_sc[f] = ug[tm - SUBLANES:]
    cv_sc[f] = uv[tm - SUBLANES:]
    g = _causal_conv(ug, pg, cwg_ref[...], cbg_ref[...])
    v = _causal_conv(uv, pv, cwv_ref[...], cbv_ref[...])
    a = (_gelu_tanh(g) * v).astype(BF16)
    part = jnp.dot(a, wd_ref[...], preferred_element_type=F32)

    @pl.when(f == 0)
    def _():
        acc_sc[...] = part

    @pl.when(f > 0)
    def _():
        acc_sc[...] += part

    @pl.when(f == nf - 1)
    def _():
        o_ref[0] = x_ref[0] + _rmsnorm(acc_sc[...], wpost_ref[...])


def _ffn_layer(x, w_pre, w_post, w_up, conv_w, conv_b, w_down):
    bsz, s, d = x.shape
    d_ff = w_down.shape[0]
    kc = conv_w.shape[0]
    tm, fc = FFN_TM, FFN_FC
    nf = d_ff // fc
    grid = (bsz, s // tm, nf)
    row = lambda v: v.reshape(1, -1)
    const = lambda shape: pl.BlockSpec(shape, lambda b, t, f: (0, 0))
    in_specs = [
        pl.BlockSpec((1, tm, d), lambda b, t, f: (b, t, 0)),
        const((1, d)),
        pl.BlockSpec((d, fc), lambda b, t, f: (0, f)),
        pl.BlockSpec((d, fc), lambda b, t, f: (0, f + nf)),
        pl.BlockSpec((kc, fc), lambda b, t, f: (0, f)),
        pl.BlockSpec((kc, fc), lambda b, t, f: (0, f + nf)),
        pl.BlockSpec((1, fc), lambda b, t, f: (0, f)),
        pl.BlockSpec((1, fc), lambda b, t, f: (0, f + nf)),
        pl.BlockSpec((fc, d), lambda b, t, f: (f, 0)),
        const((1, d)),
    ]
    w_up_b = w_up.astype(BF16)
    return pl.pallas_call(
        _ffn_kernel,
        grid=grid,
        in_specs=in_specs,
        out_specs=pl.BlockSpec((1, tm, d), lambda b, t, f: (b, t, 0)),
        out_shape=jax.ShapeDtypeStruct(x.shape, x.dtype),
        scratch_shapes=[
            pltpu.VMEM((tm, d), BF16),
            pltpu.VMEM((tm, d), F32),
            pltpu.VMEM((nf, SUBLANES, fc), F32),
            pltpu.VMEM((nf, SUBLANES, fc), F32),
        ],
        compiler_params=pltpu.CompilerParams(
            dimension_semantics=("arbitrary", "arbitrary", "arbitrary"),
            vmem_limit_bytes=VMEM_LIMIT_BYTES),
        name="ffn_layer",
    )(x, row(w_pre), w_up_b, w_up_b, conv_w, conv_w, row(conv_b), row(conv_b),
      w_down.astype(BF16), row(w_post))


def _lru_kernel(x_ref, wpre_ref, win_ref, bin_ref, cw_ref, cb_ref, wg_ref, bgx_ref, bga_ref, lam_ref,
                wout_ref, bout_ref, wpost_ref, o_ref, cx_sc, hc_sc):
    t = pl.program_id(1)
    tl = x_ref.shape[1]
    width = cw_ref.shape[1]
    nh = wg_ref.shape[0]
    blk = wg_ref.shape[1]

    @pl.when(t == 0)
    def _():
        cx_sc[...] = jnp.zeros(cx_sc.shape, F32)
        hc_sc[...] = jnp.zeros(hc_sc.shape, F32)

    x = x_ref[0]
    h = _rmsnorm(x, wpre_ref[...]).astype(BF16)
    proj = jnp.dot(h, win_ref[...], preferred_element_type=F32) + bin_ref[...]
    ybr = _gelu_tanh(proj[:, :width])
    xraw = proj[:, width:]
    prev = cx_sc[...]
    cx_sc[...] = xraw[tl - SUBLANES:]
    xbr = _causal_conv(xraw, prev, cw_ref[...], cb_ref[...])

    xb = xbr.astype(BF16)
    gx_parts, ga_parts = [], []
    for hd in range(nh):
        gg = jnp.dot(xb[:, hd * blk:(hd + 1) * blk], wg_ref[hd], preferred_element_type=F32)
        gx_parts.append(gg[:, :blk])
        ga_parts.append(gg[:, blk:])
    gx = _sigmoid(jnp.concatenate(gx_parts, axis=1) + bgx_ref[...])
    ga = _sigmoid(jnp.concatenate(ga_parts, axis=1) + bga_ref[...])
    log_sig_lam = -_softplus(-lam_ref[...])
    log_a = LRU_C * ga * log_sig_lam
    a_t = jnp.exp(log_a)
    mult = jnp.sqrt(-jnp.tanh(log_a) * (a_t * a_t + 1.0))
    b_t = mult * gx * xbr

    rows = lax.broadcasted_iota(jnp.int32, (SUBLANES, width), 0)
    carry = hc_sc[...]
    outs = []
    for i in range(tl // SUBLANES):
        a = a_t[i * SUBLANES:(i + 1) * SUBLANES]
        b = b_t[i * SUBLANES:(i + 1) * SUBLANES]
        for sft in (1, 2, 4):
            m = rows >= sft
            a_sh = pltpu.roll(a, sft, 0)
            b_sh = pltpu.roll(b, sft, 0)
            b = jnp.where(m, a * b_sh + b, b)
            a = jnp.where(m, a * a_sh, a)
        hs = b + a * carry
        outs.append(hs)
        carry = jnp.broadcast_to(hs[SUBLANES - 1:SUBLANES], (SUBLANES, width))
    hc_sc[...] = carry
    hseq = jnp.concatenate(outs, axis=0)

    out = jnp.dot((hseq * ybr).astype(BF16), wout_ref[...], preferred_element_type=F32) + bout_ref[...]
    o_ref[0] = x + _rmsnorm(out, wpost_ref[...])


def _lru_layer(x, w_pre, w_post, w_in, b_in, conv_w, conv_b, w_gx, b_gx, w_ga, b_ga, lam, w_out, b_out):
    bsz, s, d = x.shape
    width = conv_w.shape[1]
    kc = conv_w.shape[0]
    nh, blk = w_gx.shape[0], w_gx.shape[1]
    tl = LRU_TL
    grid = (bsz, s // tl)
    row = lambda v: v.reshape(1, -1)
    const2 = lambda shape: pl.BlockSpec(shape, lambda b, t: (0, 0))
    w_g = jnp.concatenate([w_gx, w_ga], axis=-1).astype(BF16)
    in_specs = [
        pl.BlockSpec((1, tl, d), lambda b, t: (b, t, 0)),
        const2((1, d)),
        const2((d, 2 * width)),
        const2((1, 2 * width)),
        const2((kc, width)),
        const2((1, width)),
        pl.BlockSpec((nh, blk, 2 * blk), lambda b, t: (0, 0, 0)),
        const2((1, width)),
        const2((1, width)),
        const2((1, width)),
        const2((width, d)),
        const2((1, d)),
        const2((1, d)),
    ]
    return pl.pallas_call(
        _lru_kernel,
        grid=grid,
        in_specs=in_specs,
        out_specs=pl.BlockSpec((1, tl, d), lambda b, t: (b, t, 0)),
        out_shape=jax.ShapeDtypeStruct(x.shape, x.dtype),
        scratch_shapes=[
            pltpu.VMEM((SUBLANES, width), F32),
            pltpu.VMEM((SUBLANES, width), F32),
        ],
        compiler_params=pltpu.CompilerParams(
            dimension_semantics=("arbitrary", "arbitrary"),
            vmem_limit_bytes=VMEM_LIMIT_BYTES),
        name="lru_layer",
    )(x, row(w_pre), w_in.astype(BF16), row(b_in), conv_w, row(conv_b), w_g, row(b_gx), row(b_ga),
      row(lam), w_out.astype(BF16), row(b_out), row(w_post))


def _ssd_kernel(x_ref, wpre_ref, wz_ref, wxbc_ref, wdt_ref, cw_ref, cb_ref, dtb_ref, alog_ref, drep_ref,
                nw_ref, wout_ref, wpost_ref, o_ref, cx_sc, st_sc):
    t = pl.program_id(1)
    L = x_ref.shape[1]
    d_inner = wz_ref.shape[1]
    n = SSD_D_STATE
    gn = SSD_GROUPS * n
    nheads = dtb_ref.shape[1]
    hpg = nheads // SSD_GROUPS
    p = SSD_HEADDIM
    hp = jax.lax.Precision.HIGHEST

    @pl.when(t == 0)
    def _():
        cx_sc[...] = jnp.zeros(cx_sc.shape, F32)
        st_sc[...] = jnp.zeros(st_sc.shape, F32)

    x = x_ref[0]
    h = _rmsnorm(x, wpre_ref[...]).astype(BF16)
    xraw = jnp.dot(h, wxbc_ref[...], preferred_element_type=F32)
    prev = cx_sc[...]
    cx_sc[...] = xraw[L - SUBLANES:]
    xc = _causal_conv(xraw, prev, cw_ref[...], cb_ref[...])
    xbc = xc * _sigmoid(xc)
    xs = xbc[:, :d_inner]
    bm = xbc[:, d_inner:d_inner + gn]
    cm = xbc[:, d_inner + gn:]

    dt = _softplus(jnp.dot(h, wdt_ref[...], preferred_element_type=F32) + dtb_ref[...])
    da = dt * (-jnp.exp(alog_ref[...]))
    ri = lax.broadcasted_iota(jnp.int32, (L, L), 0)
    ci = lax.broadcasted_iota(jnp.int32, (L, L), 1)
    tri = ri >= ci
    tril = tri.astype(F32)
    cs = jnp.dot(tril, da, precision=hp, preferred_element_type=F32)
    cs_t = lax.dot_general(da, (ri <= ci).astype(F32), (((0,), (0,)), ((), ())),
                           precision=hp, preferred_element_type=F32)
    dt_t = dt.T
    tot = cs[L - 1:L, :]
    lane = lax.broadcasted_iota(jnp.int32, (1, 2 * p), 1)
    lo = lane < p

    y_parts = []
    for g in range(SSD_GROUPS):
        bg = bm[:, g * n:(g + 1) * n]
        cg = cm[:, g * n:(g + 1) * n]
        cb = lax.dot_general(cg.astype(BF16), bg.astype(BF16), (((1,), (1,)), ((), ())),
                             preferred_element_type=F32)
        for q in range(hpg // 2):
            pair = g * (hpg // 2) + q
            xs_q = xs[:, pair * 2 * p:(pair + 1) * 2 * p]
            st_q = st_sc[pair]
            rhs = jnp.concatenate([xs_q, st_q], axis=0)
            y_q = None
            upd = None
            for e in range(2):
                hd = 2 * pair + e
                keep = lo if e == 0 else jnp.logical_not(lo)
                col = cs[:, hd:hd + 1]
                diff = col - cs_t[hd:hd + 1, :]
                dec = jnp.exp(jnp.where(tri, diff, -jnp.inf))
                m = cb * dec * dt_t[hd:hd + 1, :]
                ce = cg * jnp.exp(col)
                lhs = jnp.concatenate([m, ce], axis=1).astype(BF16)
                rhs_e = jnp.where(keep, rhs, 0.0).astype(BF16)
                y_e = jnp.dot(lhs, rhs_e, preferred_element_type=F32)
                y_q = y_e if y_q is None else y_q + y_e
                wcol = dt[:, hd:hd + 1] * jnp.exp(tot[:, hd:hd + 1] - col)
                bw = (bg * wcol).astype(BF16)
                xs_e = jnp.where(keep, xs_q, 0.0).astype(BF16)
                u_e = lax.dot_general(bw, xs_e, (((0,), (0,)), ((), ())),
                                      preferred_element_type=F32)
                upd = u_e if upd is None else upd + u_e
            et = jnp.where(lo, jnp.exp(tot[:, 2 * pair:2 * pair + 1]),
                           jnp.exp(tot[:, 2 * pair + 1:2 * pair + 2]))
            st_sc[pair] = st_q * et + upd
            y_parts.append(y_q)
    y = jnp.concatenate(y_parts, axis=1) + drep_ref[...] * xs

    z = jnp.dot(h, wz_ref[...], preferred_element_type=F32)
    gz = y * (z * _sigmoid(z))
    gw = d_inner // SSD_GROUPS
    normed = []
    for g in range(SSD_GROUPS):
        part = gz[:, g * gw:(g + 1) * gw]
        normed.append(part * lax.rsqrt(jnp.mean(part * part, axis=-1, keepdims=True) + EPS))
    gnorm = jnp.concatenate(normed, axis=1) * nw_ref[...]
    out = jnp.dot(gnorm.astype(BF16), wout_ref[...], preferred_element_type=F32)
    o_ref[0] = x + _rmsnorm(out, wpost_ref[...])


def _ssd_layer(x, w_pre, w_post, w_in, conv_w, conv_b, dt_bias, a_log, d_skip, norm_w, w_out):
    bsz, s, d = x.shape
    d_inner = w_out.shape[0]
    nheads = dt_bias.shape[0]
    conv_dim = conv_w.shape[1]
    kc = conv_w.shape[0]
    L = SSD_CHUNK
    grid = (bsz, s // L)
    row = lambda v: v.reshape(1, -1)
    const2 = lambda shape: pl.BlockSpec(shape, lambda b, t: (0, 0))
    w_z = w_in[:, :d_inner].astype(BF16)
    w_xbc = w_in[:, d_inner:d_inner + conv_dim].astype(BF16)
    w_dt = w_in[:, d_inner + conv_dim:].astype(BF16)
    d_rep = jnp.repeat(d_skip, d_inner // nheads)
    in_specs = [
        pl.BlockSpec((1, L, d), lambda b, t: (b, t, 0)),
        const2((1, d)),
        const2((d, d_inner)),
        const2((d, conv_dim)),
        const2((d, nheads)),
        const2((kc, conv_dim)),
        const2((1, conv_dim)),
        const2((1, nheads)),
        const2((1, nheads)),
        const2((1, d_inner)),
        const2((1, d_inner)),
        const2((d_inner, d)),
        const2((1, d)),
    ]
    return pl.pallas_call(
        _ssd_kernel,
        grid=grid,
        in_specs=in_specs,
        out_specs=pl.BlockSpec((1, L, d), lambda b, t: (b, t, 0)),
        out_shape=jax.ShapeDtypeStruct(x.shape, x.dtype),
        scratch_shapes=[
            pltpu.VMEM((SUBLANES, conv_dim), F32),
            pltpu.VMEM((nheads // 2, SSD_D_STATE, 2 * SSD_HEADDIM), F32),
        ],
        compiler_params=pltpu.CompilerParams(
            dimension_semantics=("arbitrary", "arbitrary"),
            vmem_limit_bytes=VMEM_LIMIT_BYTES),
        name="ssd_layer",
    )(x, row(w_pre), w_z, w_xbc, w_dt, conv_w, row(conv_b), row(dt_bias), row(a_log), row(d_rep),
      row(norm_w), w_out.astype(BF16), row(w_post))


def kernel(x, norm_mix_pre, norm_mix_post, norm_ffn_pre, norm_ffn_post, ssd_w_in, ssd_conv_w, ssd_conv_b, ssd_dt_bias, ssd_a_log, ssd_d, ssd_norm, ssd_w_out, lru_w_in, lru_b_in, lru_conv_w, lru_conv_b, lru_w_gx, lru_b_gx, lru_w_ga, lru_b_ga, lru_lambda, lru_w_out, lru_b_out, ffn_w_up, ffn_conv_w, ffn_conv_b, ffn_w_down):
    depth = norm_mix_pre.shape[0]
    for i in range(depth):
        j = i // 2
        if i % 2 == 0:
            x = _ssd_layer(x, norm_mix_pre[i], norm_mix_post[i], ssd_w_in[j], ssd_conv_w[j], ssd_conv_b[j],
                           ssd_dt_bias[j], ssd_a_log[j], ssd_d[j], ssd_norm[j], ssd_w_out[j])
        else:
            x = _lru_layer(x, norm_mix_pre[i], norm_mix_post[i], lru_w_in[j], lru_b_in[j], lru_conv_w[j],
                           lru_conv_b[j], lru_w_gx[j], lru_b_gx[j], lru_w_ga[j], lru_b_ga[j], lru_lambda[j],
                           lru_w_out[j], lru_b_out[j])
        x = _ffn_layer(x, norm_ffn_pre[i], norm_ffn_post[i], ffn_w_up[i], ffn_conv_w[i], ffn_conv_b[i],
                       ffn_w_down[i])
    return x
```

```python
import math

import jax
import jax.numpy as jnp
from jax import lax
from jax.experimental import pallas as pl
from jax.experimental.pallas import tpu as pltpu

F32 = jnp.float32
BF16 = jnp.bfloat16
EPS = 1e-6

SUBLANES = 8
LANES = 128
VMEM_LIMIT_BYTES = 56 * 1024 * 1024

SSD_HEADDIM = 64
SSD_GROUPS = 4
SSD_D_STATE = 128
SSD_CHUNK = 128
LRU_C = 8.0

FFN_TM = 512
FFN_FC = 512
LRU_TL = 64
SSD_TS = 512
SSD_CONV_BLOCK = 512


def _rmsnorm(x, w):
    ms = jnp.mean(x * x, axis=-1, keepdims=True)
    return x * lax.rsqrt(ms + EPS) * w


def _gelu_tanh(x):
    c = math.sqrt(2.0 / math.pi)
    return x * (0.5 * (1.0 + jnp.tanh(c * (x + 0.044715 * (x * x * x)))))


def _sigmoid(x):
    return 1.0 / (1.0 + jnp.exp(-x))


def _softplus(x):
    return jnp.maximum(x, 0.0) + jnp.log1p(jnp.exp(-jnp.abs(x)))


def _conv_via_slab(u, slab_ref, carry_ref, w, b, nseg=1):
    total, c = u.shape
    rows = total // nseg
    pitch = SUBLANES + rows
    k = w.shape[0]
    outs = []
    for j in range(c // LANES):
        ls = slice(j * LANES, (j + 1) * LANES)
        uj = u[:, ls]
        for s in range(nseg):
            seg = uj[s * rows:(s + 1) * rows]
            slab_ref[j, s * pitch:s * pitch + SUBLANES, :] = carry_ref[j, s * SUBLANES:(s + 1) * SUBLANES, :]
            slab_ref[j, s * pitch + SUBLANES:(s + 1) * pitch, :] = seg
            carry_ref[j, s * SUBLANES:(s + 1) * SUBLANES, :] = seg[rows - SUBLANES:]
        y = b[:, ls]
        for tap_i in range(k):
            shift = k - 1 - tap_i
            if shift == 0:
                tap = uj
            else:
                tap = jnp.concatenate(
                    [slab_ref[j, s * pitch + SUBLANES - shift:(s + 1) * pitch - shift, :] for s in range(nseg)],
                    axis=0)
            y = y + tap * w[tap_i:tap_i + 1, ls]
        outs.append(y)
    return jnp.concatenate(outs, axis=1)


def _ffn_kernel(x_ref, wpre_ref, wup_ref, cw_ref, cb_ref, wd_ref, wpost_ref, o_ref, cg_sc, cv_sc, slab_sc):
    t = pl.program_id(1)
    d_ff = wd_ref.shape[0]
    fc = cg_sc.shape[1] * LANES
    nf = d_ff // fc

    @pl.when(t == 0)
    def _():
        cg_sc[...] = jnp.zeros(cg_sc.shape, F32)
        cv_sc[...] = jnp.zeros(cv_sc.shape, F32)

    x = x_ref[0]
    h = _rmsnorm(x, wpre_ref[...]).astype(BF16)
    acc = None
    for f in range(nf):
        gs = slice(f * fc, (f + 1) * fc)
        vs = slice(d_ff + f * fc, d_ff + (f + 1) * fc)
        ug = jnp.dot(h, wup_ref[:, gs], preferred_element_type=F32)
        uv = jnp.dot(h, wup_ref[:, vs], preferred_element_type=F32)
        g = _conv_via_slab(ug, slab_sc.at[f % 2, 0], cg_sc.at[f], cw_ref[:, gs], cb_ref[:, gs])
        v = _conv_via_slab(uv, slab_sc.at[f % 2, 1], cv_sc.at[f], cw_ref[:, vs], cb_ref[:, vs])
        a = (_gelu_tanh(g) * v).astype(BF16)
        part = jnp.dot(a, wd_ref[gs, :], preferred_element_type=F32)
        acc = part if acc is None else acc + part
    o_ref[0] = x + _rmsnorm(acc, wpost_ref[...])


def _ffn_layer(x, w_pre, w_post, w_up, conv_w, conv_b, w_down):
    bsz, s, d = x.shape
    d_ff = w_down.shape[0]
    kc = conv_w.shape[0]
    tm, fc = FFN_TM, FFN_FC
    nf = d_ff // fc
    grid = (bsz, s // tm)
    row = lambda v: v.reshape(1, -1)
    const = lambda shape: pl.BlockSpec(shape, lambda b, t: (0, 0), pipeline_mode=pl.Buffered(1))
    in_specs = [
        pl.BlockSpec((1, tm, d), lambda b, t: (b, t, 0)),
        const((1, d)),
        const((d, 2 * d_ff)),
        const((kc, 2 * d_ff)),
        const((1, 2 * d_ff)),
        const((d_ff, d)),
        const((1, d)),
    ]
    return pl.pallas_call(
        _ffn_kernel,
        grid=grid,
        in_specs=in_specs,
        out_specs=pl.BlockSpec((1, tm, d), lambda b, t: (b, t, 0)),
        out_shape=jax.ShapeDtypeStruct(x.shape, x.dtype),
        scratch_shapes=[
            pltpu.VMEM((nf, fc // LANES, SUBLANES, LANES), F32),
            pltpu.VMEM((nf, fc // LANES, SUBLANES, LANES), F32),
            pltpu.VMEM((2, 2, fc // LANES, SUBLANES + tm, LANES), F32),
        ],
        compiler_params=pltpu.CompilerParams(
            dimension_semantics=("arbitrary", "arbitrary"),
            vmem_limit_bytes=VMEM_LIMIT_BYTES),
        name="ffn_layer",
    )(x, row(w_pre), w_up.astype(BF16), conv_w, row(conv_b), w_down.astype(BF16), row(w_post))


def _lru_kernel(x_ref, wpre_ref, win_ref, bin_ref, cw_ref, cb_ref, wg_ref, bgx_ref, bga_ref, lam_ref,
                wout_ref, bout_ref, wpost_ref, o_ref, cx_sc, hc_sc, slab_sc, a_sc, b_sc, h_sc):
    t = pl.program_id(0)
    nb, tl, d = x_ref.shape
    width = cw_ref.shape[1]
    nh = wg_ref.shape[0]
    blk = wg_ref.shape[1]
    nslab = width // LANES
    pitch = a_sc.shape[1] // nb

    @pl.when(t == 0)
    def _():
        cx_sc[...] = jnp.zeros(cx_sc.shape, F32)
        hc_sc[...] = jnp.zeros(hc_sc.shape, F32)

    x = x_ref[...].reshape(nb * tl, d)
    h = _rmsnorm(x, wpre_ref[...]).astype(BF16)
    proj = jnp.dot(h, win_ref[...], preferred_element_type=F32) + bin_ref[...]
    ybr = _gelu_tanh(proj[:, :width])
    xbr = _conv_via_slab(proj[:, width:], slab_sc, cx_sc, cw_ref[...], cb_ref[...], nseg=nb)

    xb = xbr.astype(BF16)
    gx_parts, ga_parts = [], []
    for hd in range(nh):
        gg = jnp.dot(xb[:, hd * blk:(hd + 1) * blk], wg_ref[hd], preferred_element_type=F32)
        gx_parts.append(gg[:, :blk])
        ga_parts.append(gg[:, blk:])
    gx = _sigmoid(jnp.concatenate(gx_parts, axis=1) + bgx_ref[...])
    ga = _sigmoid(jnp.concatenate(ga_parts, axis=1) + bga_ref[...])
    log_sig_lam = -_softplus(-lam_ref[...])
    log_a = LRU_C * ga * log_sig_lam
    a_t = jnp.exp(log_a)
    mult = jnp.sqrt(-jnp.tanh(log_a) * (a_t * a_t + 1.0))
    b_t = mult * gx * xbr

    for j in range(nslab):
        ls = slice(j * LANES, (j + 1) * LANES)
        for bi in range(nb):
            a_sc[j, bi * pitch:bi * pitch + tl, :] = a_t[bi * tl:(bi + 1) * tl, ls]
            b_sc[j, bi * pitch:bi * pitch + tl, :] = b_t[bi * tl:(bi + 1) * tl, ls]
    hcur = [hc_sc[j] for j in range(nslab)]
    for step in range(tl):
        for j in range(nslab):
            a_v = a_sc[j, pl.ds(step, nb, stride=pitch), :]
            b_v = b_sc[j, pl.ds(step, nb, stride=pitch), :]
            hcur[j] = a_v * hcur[j] + b_v
            h_sc[j, pl.ds(step, nb, stride=pitch), :] = hcur[j]
    for j in range(nslab):
        hc_sc[j] = hcur[j]
    hseq = jnp.concatenate(
        [jnp.concatenate([h_sc[j, bi * pitch:bi * pitch + tl, :] for bi in range(nb)], axis=0)
         for j in range(nslab)], axis=1)

    out = jnp.dot((hseq * ybr).astype(BF16), wout_ref[...], preferred_element_type=F32) + bout_ref[...]
    o_ref[...] = (x + _rmsnorm(out, wpost_ref[...])).reshape(nb, tl, d)


def _lru_layer(x, w_pre, w_post, w_in, b_in, conv_w, conv_b, w_gx, b_gx, w_ga, b_ga, lam, w_out, b_out):
    bsz, s, d = x.shape
    width = conv_w.shape[1]
    kc = conv_w.shape[0]
    nh, blk = w_gx.shape[0], w_gx.shape[1]
    tl = LRU_TL
    nslab = width // LANES
    pitch = tl + SUBLANES
    grid = (s // tl,)
    row = lambda v: v.reshape(1, -1)
    const2 = lambda shape: pl.BlockSpec(shape, lambda t: (0, 0), pipeline_mode=pl.Buffered(1))
    w_g = jnp.concatenate([w_gx, w_ga], axis=-1).astype(BF16)
    in_specs = [
        pl.BlockSpec((bsz, tl, d), lambda t: (0, t, 0)),
        const2((1, d)),
        const2((d, 2 * width)),
        const2((1, 2 * width)),
        const2((kc, width)),
        const2((1, width)),
        pl.BlockSpec((nh, blk, 2 * blk), lambda t: (0, 0, 0), pipeline_mode=pl.Buffered(1)),
        const2((1, width)),
        const2((1, width)),
        const2((1, width)),
        const2((width, d)),
        const2((1, d)),
        const2((1, d)),
    ]
    return pl.pallas_call(
        _lru_kernel,
        grid=grid,
        in_specs=in_specs,
        out_specs=pl.BlockSpec((bsz, tl, d), lambda t: (0, t, 0)),
        out_shape=jax.ShapeDtypeStruct(x.shape, x.dtype),
        scratch_shapes=[
            pltpu.VMEM((nslab, bsz * SUBLANES, LANES), F32),
            pltpu.VMEM((nslab, bsz, LANES), F32),
            pltpu.VMEM((nslab, bsz * pitch, LANES), F32),
            pltpu.VMEM((nslab, bsz * pitch, LANES), F32),
            pltpu.VMEM((nslab, bsz * pitch, LANES), F32),
            pltpu.VMEM((nslab, bsz * pitch, LANES), F32),
        ],
        compiler_params=pltpu.CompilerParams(
            dimension_semantics=("arbitrary",),
            vmem_limit_bytes=VMEM_LIMIT_BYTES),
        name="lru_layer",
    )(x, row(w_pre), w_in.astype(BF16), row(b_in), conv_w, row(conv_b), w_g, row(b_gx), row(b_ga),
      row(lam), w_out.astype(BF16), row(b_out), row(w_post))


def _ssd_kernel(x_ref, wpre_ref, wz_ref, wxbc_ref, wdt_ref, cw_ref, cb_ref, dtb_ref, alog_ref, drep_ref,
                nw_ref, wout_ref, wpost_ref, o_ref, cx_sc, st_sc, slab_sc, xbc_sc):
    t = pl.program_id(1)
    ts = x_ref.shape[1]
    L = SSD_CHUNK
    d_inner = wz_ref.shape[1]
    conv_dim = wxbc_ref.shape[1]
    n = SSD_D_STATE
    gn = SSD_GROUPS * n
    nheads = dtb_ref.shape[1]
    ppg = nheads // SSD_GROUPS // 2
    p = SSD_HEADDIM
    cblk = slab_sc.shape[1] * LANES
    nsl = cblk // LANES
    hp = jax.lax.Precision.HIGHEST
    log2e = math.log2(math.e)

    @pl.when(t == 0)
    def _():
        cx_sc[...] = jnp.zeros(cx_sc.shape, F32)
        st_sc[...] = jnp.zeros(st_sc.shape, F32)

    x = x_ref[0]
    h = _rmsnorm(x, wpre_ref[...]).astype(BF16)

    for j in range(conv_dim // cblk):
        cols = slice(j * cblk, (j + 1) * cblk)
        raw = jnp.dot(h, wxbc_ref[:, cols], preferred_element_type=F32)
        xc = _conv_via_slab(raw, slab_sc.at[j % 2], cx_sc.at[j * nsl:(j + 1) * nsl], cw_ref[:, cols],
                            cb_ref[:, cols])
        xbc_sc[:, cols] = xc * _sigmoid(xc)

    z = jnp.dot(h, wz_ref[...], preferred_element_type=F32)
    dt_all = _softplus(jnp.dot(h, wdt_ref[...], preferred_element_type=F32) + dtb_ref[...])
    da_all = dt_all * (-jnp.exp(alog_ref[...]))

    ri = lax.broadcasted_iota(jnp.int32, (L, L), 0)
    ci = lax.broadcasted_iota(jnp.int32, (L, L), 1)
    tri = ri >= ci
    tril = tri.astype(F32)
    triu = (ri <= ci).astype(F32)
    lo = lax.broadcasted_iota(jnp.int32, (1, 2 * p), 1) < p

    nch = ts // L
    da_cat = jnp.concatenate([da_all[c * L:(c + 1) * L] for c in range(nch)], axis=1)
    dt_cat = jnp.concatenate([dt_all[c * L:(c + 1) * L] for c in range(nch)], axis=1)
    cs_cat = jnp.dot(tril, da_cat, precision=hp, preferred_element_type=F32) * log2e
    cs_t_cat = lax.dot_general(da_cat, triu, (((0,), (0,)), ((), ())), precision=hp,
                               preferred_element_type=F32) * log2e
    dt_t_cat = dt_cat.T
    src_t_cat = cs_t_cat - jnp.log2(dt_t_cat)
    w_t_cat = dt_t_cat * jnp.exp2(cs_t_cat[:, L - 1:L] - cs_t_cat)
    et_cat = jnp.exp2(cs_cat[L - 1:L, :])

    gnorm_parts = []
    for c in range(nch):
        rows = slice(c * L, (c + 1) * L)
        hs = slice(c * nheads, (c + 1) * nheads)
        cs = cs_cat[:, hs]
        src_t = src_t_cat[hs]
        w_t = w_t_cat[hs]
        et = et_cat[:, hs]
        y_parts = []
        for g in range(SSD_GROUPS):
            bg = xbc_sc[rows, d_inner + g * n:d_inner + (g + 1) * n]
            cg = xbc_sc[rows, d_inner + gn + g * n:d_inner + gn + (g + 1) * n]
            cb = lax.dot_general(cg.astype(BF16), bg.astype(BF16), (((1,), (1,)), ((), ())),
                                 preferred_element_type=F32)
            bg_t = bg.T
            for q in range(ppg):
                pair = g * ppg + q
                xs_q = xbc_sc[rows, pair * 2 * p:(pair + 1) * 2 * p]
                st_q = st_sc[pair]
                rhs = jnp.concatenate([xs_q, st_q], axis=0).astype(BF16)
                ys, us = [], []
                for e in range(2):
                    hd = 2 * pair + e
                    colb = jnp.broadcast_to(cs[:, hd:hd + 1], (L, L))
                    m = cb * jnp.exp2(jnp.where(tri, colb - src_t[hd:hd + 1, :], -jnp.inf))
                    ce = cg * jnp.exp2(colb)
                    lhs = jnp.concatenate([m, ce], axis=1).astype(BF16)
                    ys.append(jnp.dot(lhs, rhs, preferred_element_type=F32))
                    bw = (bg_t * w_t[hd:hd + 1, :]).astype(BF16)
                    us.append(jnp.dot(bw, rhs[:L], preferred_element_type=F32))
                y_parts.append(jnp.where(lo, ys[0], ys[1]))
                et_q = jnp.where(lo, et[:, 2 * pair:2 * pair + 1], et[:, 2 * pair + 1:2 * pair + 2])
                st_sc[pair] = st_q * et_q + jnp.where(lo, us[0], us[1])
        xs = xbc_sc[rows, :d_inner]
        y = jnp.concatenate(y_parts, axis=1) + drep_ref[...] * xs
        zc = z[rows]
        gz = y * (zc * _sigmoid(zc))
        gw = d_inner // SSD_GROUPS
        normed = []
        for g in range(SSD_GROUPS):
            part = gz[:, g * gw:(g + 1) * gw]
            normed.append(part * lax.rsqrt(jnp.mean(part * part, axis=-1, keepdims=True) + EPS))
        gnorm_parts.append((jnp.concatenate(normed, axis=1) * nw_ref[...]).astype(BF16))
    gnorm = jnp.concatenate(gnorm_parts, axis=0)
    out = jnp.dot(gnorm, wout_ref[...], preferred_element_type=F32)
    o_ref[0] = x + _rmsnorm(out, wpost_ref[...])


def _ssd_layer(x, w_pre, w_post, w_in, conv_w, conv_b, dt_bias, a_log, d_skip, norm_w, w_out):
    bsz, s, d = x.shape
    d_inner = w_out.shape[0]
    nheads = dt_bias.shape[0]
    conv_dim = conv_w.shape[1]
    kc = conv_w.shape[0]
    ts = SSD_TS
    cblk = SSD_CONV_BLOCK
    grid = (bsz, s // ts)
    row = lambda v: v.reshape(1, -1)
    const2 = lambda shape: pl.BlockSpec(shape, lambda b, t: (0, 0), pipeline_mode=pl.Buffered(1))
    w_z = w_in[:, :d_inner].astype(BF16)
    w_xbc = w_in[:, d_inner:d_inner + conv_dim].astype(BF16)
    w_dt = w_in[:, d_inner + conv_dim:].astype(BF16)
    d_rep = jnp.repeat(d_skip, d_inner // nheads)
    in_specs = [
        pl.BlockSpec((1, ts, d), lambda b, t: (b, t, 0)),
        const2((1, d)),
        const2((d, d_inner)),
        const2((d, conv_dim)),
        const2((d, nheads)),
        const2((kc, conv_dim)),
        const2((1, conv_dim)),
        const2((1, nheads)),
        const2((1, nheads)),
        const2((1, d_inner)),
        const2((1, d_inner)),
        const2((d_inner, d)),
        const2((1, d)),
    ]
    return pl.pallas_call(
        _ssd_kernel,
        grid=grid,
        in_specs=in_specs,
        out_specs=pl.BlockSpec((1, ts, d), lambda b, t: (b, t, 0)),
        out_shape=jax.ShapeDtypeStruct(x.shape, x.dtype),
        scratch_shapes=[
            pltpu.VMEM((conv_dim // LANES, SUBLANES, LANES), F32),
            pltpu.VMEM((nheads // 2, SSD_D_STATE, 2 * SSD_HEADDIM), F32),
            pltpu.VMEM((2, cblk // LANES, SUBLANES + ts, LANES), F32),
            pltpu.VMEM((ts, conv_dim), F32),
        ],
        compiler_params=pltpu.CompilerParams(
            dimension_semantics=("arbitrary", "arbitrary"),
            vmem_limit_bytes=VMEM_LIMIT_BYTES),
        name="ssd_layer",
    )(x, row(w_pre), w_z, w_xbc, w_dt, conv_w, row(conv_b), row(dt_bias), row(a_log), row(d_rep),
      row(norm_w), w_out.astype(BF16), row(w_post))


def kernel(x, norm_mix_pre, norm_mix_post, norm_ffn_pre, norm_ffn_post, ssd_w_in, ssd_conv_w, ssd_conv_b, ssd_dt_bias, ssd_a_log, ssd_d, ssd_norm, ssd_w_out, lru_w_in, lru_b_in, lru_conv_w, lru_conv_b, lru_w_gx, lru_b_gx, lru_w_ga, lru_b_ga, lru_lambda, lru_w_out, lru_b_out, ffn_w_up, ffn_conv_w, ffn_conv_b, ffn_w_down):
    depth = norm_mix_pre.shape[0]
    for i in range(depth):
        j = i // 2
        if i % 2 == 0:
            x = _ssd_layer(x, norm_mix_pre[i], norm_mix_post[i], ssd_w_in[j], ssd_conv_w[j], ssd_conv_b[j],
                           ssd_dt_bias[j], ssd_a_log[j], ssd_d[j], ssd_norm[j], ssd_w_out[j])
        else:
            x = _lru_layer(x, norm_mix_pre[i], norm_mix_post[i], lru_w_in[j], lru_b_in[j], lru_conv_w[j],
                           lru_conv_b[j], lru_w_gx[j], lru_b_gx[j], lru_w_ga[j], lru_b_ga[j], lru_lambda[j],
                           lru_w_out[j], lru_b_out[j])
        x = _ffn_layer(x, norm_ffn_pre[i], norm_ffn_post[i], ffn_w_up[i], ffn_conv_w[i], ffn_conv_b[i],
                       ffn_w_down[i])
    return x
```

```python
import math

import jax
import jax.numpy as jnp
from jax import lax
from jax.experimental import pallas as pl
from jax.experimental.pallas import tpu as pltpu

F32 = jnp.float32
BF16 = jnp.bfloat16
EPS = 1e-6

SUBLANES = 8
LANES = 128
VMEM_LIMIT_BYTES = 56 * 1024 * 1024

SSD_HEADDIM = 64
SSD_GROUPS = 4
SSD_D_STATE = 128
SSD_CHUNK = 128
LRU_C = 8.0

FFN_TM = 512
FFN_FC = 1024
LRU_TL = 64
LRU_NSUB = 2
SSD_TS = 512
SSD_CONV_BLOCK = 512


def _rmsnorm(x, w):
    ms = jnp.mean(x * x, axis=-1, keepdims=True)
    return x * lax.rsqrt(ms + EPS) * w


def _gelu_tanh(x):
    c = math.sqrt(2.0 / math.pi)
    return x * (0.5 * (1.0 + jnp.tanh(c * (x + 0.044715 * (x * x * x)))))


def _silu(x):
    hx = 0.5 * x
    return hx + hx * jnp.tanh(hx)


def _softplus(x):
    return jnp.maximum(x, 0.0) + jnp.log1p(jnp.exp(-jnp.abs(x)))


def _conv_via_slab(u, slab_ref, carry_ref, w, b, nseg=1):
    total, c = u.shape
    rows = total // nseg
    pitch = SUBLANES + rows
    k = w.shape[0]
    outs = []
    for j in range(c // LANES):
        ls = slice(j * LANES, (j + 1) * LANES)
        uj = u[:, ls]
        for s in range(nseg):
            seg = uj[s * rows:(s + 1) * rows]
            slab_ref[j, s * pitch:s * pitch + SUBLANES, :] = carry_ref[j, s * SUBLANES:(s + 1) * SUBLANES, :]
            slab_ref[j, s * pitch + SUBLANES:(s + 1) * pitch, :] = seg
            carry_ref[j, s * SUBLANES:(s + 1) * SUBLANES, :] = seg[rows - SUBLANES:]
        y = b[:, ls]
        for tap_i in range(k):
            shift = k - 1 - tap_i
            if shift == 0:
                tap = uj
            else:
                tap = jnp.concatenate(
                    [slab_ref[j, s * pitch + SUBLANES - shift:(s + 1) * pitch - shift, :] for s in range(nseg)],
                    axis=0)
            y = y + tap * w[tap_i:tap_i + 1, ls]
        outs.append(y)
    return jnp.concatenate(outs, axis=1)


def _ffn_kernel(x_ref, wpre_ref, wup_ref, cw_ref, cb_ref, wd_ref, wpost_ref, o_ref, cg_sc, cv_sc, slab_sc):
    t = pl.program_id(1)
    d_ff = wd_ref.shape[0]
    fc = cg_sc.shape[1] * LANES
    nf = d_ff // fc

    @pl.when(t == 0)
    def _():
        cg_sc[...] = jnp.zeros(cg_sc.shape, F32)
        cv_sc[...] = jnp.zeros(cv_sc.shape, F32)

    x = x_ref[0]
    h = _rmsnorm(x, wpre_ref[...]).astype(BF16)
    acc = None
    for f in range(nf):
        gs = slice(f * fc, (f + 1) * fc)
        vs = slice(d_ff + f * fc, d_ff + (f + 1) * fc)
        ug = jnp.dot(h, wup_ref[:, gs], preferred_element_type=F32)
        uv = jnp.dot(h, wup_ref[:, vs], preferred_element_type=F32)
        g = _conv_via_slab(ug, slab_sc.at[f % 2, 0], cg_sc.at[f], cw_ref[:, gs], cb_ref[:, gs])
        v = _conv_via_slab(uv, slab_sc.at[f % 2, 1], cv_sc.at[f], cw_ref[:, vs], cb_ref[:, vs])
        a = (_gelu_tanh(g) * v).astype(BF16)
        part = jnp.dot(a, wd_ref[gs, :], preferred_element_type=F32)
        acc = part if acc is None else acc + part
    o_ref[0] = x + _rmsnorm(acc, wpost_ref[...])


def _ffn_layer(x, w_pre, w_post, w_up, conv_w, conv_b, w_down):
    bsz, s, d = x.shape
    d_ff = w_down.shape[0]
    kc = conv_w.shape[0]
    tm, fc = FFN_TM, FFN_FC
    nf = d_ff // fc
    grid = (bsz, s // tm)
    row = lambda v: v.reshape(1, -1)
    const = lambda shape: pl.BlockSpec(shape, lambda b, t: (0, 0), pipeline_mode=pl.Buffered(1))
    in_specs = [
        pl.BlockSpec((1, tm, d), lambda b, t: (b, t, 0)),
        const((1, d)),
        const((d, 2 * d_ff)),
        const((kc, 2 * d_ff)),
        const((1, 2 * d_ff)),
        const((d_ff, d)),
        const((1, d)),
    ]
    return pl.pallas_call(
        _ffn_kernel,
        grid=grid,
        in_specs=in_specs,
        out_specs=pl.BlockSpec((1, tm, d), lambda b, t: (b, t, 0)),
        out_shape=jax.ShapeDtypeStruct(x.shape, x.dtype),
        scratch_shapes=[
            pltpu.VMEM((nf, fc // LANES, SUBLANES, LANES), F32),
            pltpu.VMEM((nf, fc // LANES, SUBLANES, LANES), F32),
            pltpu.VMEM((2, 2, fc // LANES, SUBLANES + tm, LANES), F32),
        ],
        compiler_params=pltpu.CompilerParams(
            dimension_semantics=("arbitrary", "arbitrary"),
            vmem_limit_bytes=VMEM_LIMIT_BYTES),
        name="ffn_layer",
    )(x, row(w_pre), w_up.astype(BF16), conv_w, row(conv_b), w_down.astype(BF16), row(w_post))


def _lru_kernel(x_ref, wpre_ref, win_ref, bin_ref, cw_ref, cb_ref, wg_ref, bgx_ref, bga_ref, lam_ref,
                wout_ref, bout_ref, wpost_ref, o_ref, cx_sc, hc_sc, slab_sc, a_sc, b_sc):
    t = pl.program_id(0)
    nb, tl_all, d = x_ref.shape
    nsub = slab_sc.shape[0]
    tl = tl_all // nsub
    width = cw_ref.shape[1]
    nh = wg_ref.shape[0]
    blk = wg_ref.shape[1]
    nslab = width // LANES
    pitch = a_sc.shape[2] // nb

    @pl.when(t == 0)
    def _():
        cx_sc[...] = jnp.zeros(cx_sc.shape, F32)
        hc_sc[...] = jnp.zeros(hc_sc.shape, F32)

    k_lam = (0.5 * LRU_C) * _softplus(-lam_ref[...])
    half_bgx = 0.5 * bgx_ref[...]
    half_bga = 0.5 * bga_ref[...]
    hcur = [hc_sc[j] for j in range(nslab)]

    xs, projs = [], []
    for s in range(nsub):
        x = x_ref[:, s * tl:(s + 1) * tl, :].reshape(nb * tl, d)
        h = _rmsnorm(x, wpre_ref[...]).astype(BF16)
        xs.append(x)
        projs.append(jnp.dot(h, win_ref[...], preferred_element_type=F32) + bin_ref[...])

    for s in range(nsub):
        x, proj = xs[s], projs[s]
        ybr = _gelu_tanh(proj[:, :width])
        xbr = _conv_via_slab(proj[:, width:], slab_sc.at[s], cx_sc, cw_ref[...], cb_ref[...], nseg=nb)

        xb = xbr.astype(BF16)
        ux_parts, ua_parts = [], []
        for hd in range(nh):
            gg = jnp.dot(xb[:, hd * blk:(hd + 1) * blk], wg_ref[hd], preferred_element_type=F32)
            ux_parts.append(gg[:, :blk])
            ua_parts.append(gg[:, blk:])
        tx = jnp.tanh(0.5 * jnp.concatenate(ux_parts, axis=1) + half_bgx)
        ta = jnp.tanh(0.5 * jnp.concatenate(ua_parts, axis=1) + half_bga)
        neg_log_a = k_lam * ta + k_lam
        a_t = jnp.exp(-neg_log_a)
        mult = jnp.sqrt(jnp.tanh(neg_log_a) * (a_t * a_t + 1.0))
        b_t = (mult * xbr) * (0.5 * tx + 0.5)

        for j in range(nslab):
            ls = slice(j * LANES, (j + 1) * LANES)
            for bi in range(nb):
                a_sc[s, j, bi * pitch:bi * pitch + tl, :] = a_t[bi * tl:(bi + 1) * tl, ls]
                b_sc[s, j, bi * pitch:bi * pitch + tl, :] = b_t[bi * tl:(bi + 1) * tl, ls]
        for step in range(tl):
            for j in range(nslab):
                a_v = a_sc[s, j, pl.ds(step, nb, stride=pitch), :]
                b_v = b_sc[s, j, pl.ds(step, nb, stride=pitch), :]
                hcur[j] = a_v * hcur[j] + b_v
                a_sc[s, j, pl.ds(step, nb, stride=pitch), :] = hcur[j]
        hseq = jnp.concatenate(
            [jnp.concatenate([a_sc[s, j, bi * pitch:bi * pitch + tl, :] for bi in range(nb)], axis=0)
             for j in range(nslab)], axis=1)

        out = jnp.dot((hseq * ybr).astype(BF16), wout_ref[...], preferred_element_type=F32) + bout_ref[...]
        o_ref[:, s * tl:(s + 1) * tl, :] = (x + _rmsnorm(out, wpost_ref[...])).reshape(nb, tl, d)
    for j in range(nslab):
        hc_sc[j] = hcur[j]


def _lru_layer(x, w_pre, w_post, w_in, b_in, conv_w, conv_b, w_gx, b_gx, w_ga, b_ga, lam, w_out, b_out):
    bsz, s, d = x.shape
    width = conv_w.shape[1]
    kc = conv_w.shape[0]
    nh, blk = w_gx.shape[0], w_gx.shape[1]
    tl, nsub = LRU_TL, LRU_NSUB
    nslab = width // LANES
    pitch = tl + SUBLANES
    grid = (s // (tl * nsub),)
    row = lambda v: v.reshape(1, -1)
    const2 = lambda shape: pl.BlockSpec(shape, lambda t: (0, 0), pipeline_mode=pl.Buffered(1))
    w_g = jnp.concatenate([w_gx, w_ga], axis=-1).astype(BF16)
    in_specs = [
        pl.BlockSpec((bsz, tl * nsub, d), lambda t: (0, t, 0)),
        const2((1, d)),
        const2((d, 2 * width)),
        const2((1, 2 * width)),
        const2((kc, width)),
        const2((1, width)),
        pl.BlockSpec((nh, blk, 2 * blk), lambda t: (0, 0, 0), pipeline_mode=pl.Buffered(1)),
        const2((1, width)),
        const2((1, width)),
        const2((1, width)),
        const2((width, d)),
        const2((1, d)),
        const2((1, d)),
    ]
    return pl.pallas_call(
        _lru_kernel,
        grid=grid,
        in_specs=in_specs,
        out_specs=pl.BlockSpec((bsz, tl * nsub, d), lambda t: (0, t, 0)),
        out_shape=jax.ShapeDtypeStruct(x.shape, x.dtype),
        scratch_shapes=[
            pltpu.VMEM((nslab, bsz * SUBLANES, LANES), F32),
            pltpu.VMEM((nslab, bsz, LANES), F32),
            pltpu.VMEM((nsub, nslab, bsz * pitch, LANES), F32),
            pltpu.VMEM((nsub, nslab, bsz * pitch, LANES), F32),
            pltpu.VMEM((nsub, nslab, bsz * pitch, LANES), F32),
        ],
        compiler_params=pltpu.CompilerParams(
            dimension_semantics=("arbitrary",),
            vmem_limit_bytes=VMEM_LIMIT_BYTES),
        name="lru_layer",
    )(x, row(w_pre), w_in.astype(BF16), row(b_in), conv_w, row(conv_b), w_g, row(b_gx), row(b_ga),
      row(lam), w_out.astype(BF16), row(b_out), row(w_post))


def _ssd_kernel(x_ref, wpre_ref, wz_ref, wxbc_ref, wdt_ref, cw_ref, cb_ref, dtb_ref, alog_ref, drep_ref,
                nw_ref, wout_ref, wpost_ref, o_ref, cx_sc, st_sc, slab_sc, xbc_sc):
    t = pl.program_id(1)
    ts = x_ref.shape[1]
    L = SSD_CHUNK
    d_inner = wz_ref.shape[1]
    conv_dim = wxbc_ref.shape[1]
    n = SSD_D_STATE
    gn = SSD_GROUPS * n
    nheads = dtb_ref.shape[1]
    ppg = nheads // SSD_GROUPS // 2
    p = SSD_HEADDIM
    cblk = slab_sc.shape[1] * LANES
    nsl = cblk // LANES
    hp = jax.lax.Precision.HIGHEST
    log2e = math.log2(math.e)

    @pl.when(t == 0)
    def _():
        cx_sc[...] = jnp.zeros(cx_sc.shape, F32)
        st_sc[...] = jnp.zeros(st_sc.shape, F32)

    x = x_ref[0]
    h = _rmsnorm(x, wpre_ref[...]).astype(BF16)

    for j in range(conv_dim // cblk):
        cols = slice(j * cblk, (j + 1) * cblk)
        raw = jnp.dot(h, wxbc_ref[:, cols], preferred_element_type=F32)
        xc = _conv_via_slab(raw, slab_sc.at[j % 2], cx_sc.at[j * nsl:(j + 1) * nsl], cw_ref[:, cols],
                            cb_ref[:, cols])
        xbc_sc[:, cols] = _silu(xc)

    dt_all = _softplus(jnp.dot(h, wdt_ref[...], preferred_element_type=F32) + dtb_ref[...])
    da_all = dt_all * (-jnp.exp(alog_ref[...]))

    ri = lax.broadcasted_iota(jnp.int32, (L, L), 0)
    ci = lax.broadcasted_iota(jnp.int32, (L, L), 1)
    tri = ri >= ci
    tril = tri.astype(F32)
    triu = (ri <= ci).astype(F32)
    lo = lax.broadcasted_iota(jnp.int32, (1, 2 * p), 1) < p

    nch = ts // L
    da_cat = jnp.concatenate([da_all[c * L:(c + 1) * L] for c in range(nch)], axis=1)
    dt_cat = jnp.concatenate([dt_all[c * L:(c + 1) * L] for c in range(nch)], axis=1)
    cs_cat = jnp.dot(tril, da_cat, precision=hp, preferred_element_type=F32) * log2e
    cs_t_cat = lax.dot_general(da_cat, triu, (((0,), (0,)), ((), ())), precision=hp,
                               preferred_element_type=F32) * log2e
    dt_t_cat = dt_cat.T
    src_t_cat = cs_t_cat - jnp.log2(dt_t_cat)
    w_t_cat = dt_t_cat * jnp.exp2(cs_t_cat[:, L - 1:L] - cs_t_cat)
    et_cat = jnp.exp2(cs_cat[L - 1:L, :])

    gw = d_inner // SSD_GROUPS
    zw = d_inner // nch
    z_parts, y_chunks = [], []
    for c in range(nch):
        rows = slice(c * L, (c + 1) * L)
        z_parts.append(jnp.dot(h, wz_ref[:, c * zw:(c + 1) * zw], preferred_element_type=F32))
        hs = slice(c * nheads, (c + 1) * nheads)
        cs = cs_cat[:, hs]
        src_t = src_t_cat[hs]
        w_t = w_t_cat[hs]
        et = et_cat[:, hs]
        y_parts = []
        for g in range(SSD_GROUPS):
            bg = xbc_sc[rows, d_inner + g * n:d_inner + (g + 1) * n]
            cg = xbc_sc[rows, d_inner + gn + g * n:d_inner + gn + (g + 1) * n]
            cb = lax.dot_general(cg.astype(BF16), bg.astype(BF16), (((1,), (1,)), ((), ())),
                                 preferred_element_type=F32)
            bg_t = bg.T
            for q in range(ppg):
                pair = g * ppg + q
                xs_q = xbc_sc[rows, pair * 2 * p:(pair + 1) * 2 * p]
                st_q = st_sc[pair]
                rhs = jnp.concatenate([xs_q, st_q], axis=0).astype(BF16)
                ys, us = [], []
                for e in range(2):
                    hd = 2 * pair + e
                    colb = jnp.broadcast_to(cs[:, hd:hd + 1], (L, L))
                    m = cb * jnp.exp2(jnp.where(tri, colb - src_t[hd:hd + 1, :], -jnp.inf))
                    ce = cg * jnp.exp2(colb)
                    lhs = jnp.concatenate([m, ce], axis=1).astype(BF16)
                    ys.append(jnp.dot(lhs, rhs, preferred_element_type=F32))
                    bw = (bg_t * w_t[hd:hd + 1, :]).astype(BF16)
                    us.append(jnp.dot(bw, rhs[:L], preferred_element_type=F32))
                y_parts.append(jnp.where(lo, ys[0], ys[1]))
                et_q = jnp.where(lo, et[:, 2 * pair:2 * pair + 1], et[:, 2 * pair + 1:2 * pair + 2])
                st_sc[pair] = st_q * et_q + jnp.where(lo, us[0], us[1])
        y_chunks.append(jnp.concatenate(y_parts, axis=1))

    z = jnp.concatenate(z_parts, axis=1)
    out = None
    for g in range(SSD_GROUPS):
        cols = slice(g * gw, (g + 1) * gw)
        y = jnp.concatenate([yc[:, cols] for yc in y_chunks], axis=0) + drep_ref[:, cols] * xbc_sc[:, cols]
        gz = y * _silu(z[:, cols])
        gnorm = gz * lax.rsqrt(jnp.mean(gz * gz, axis=-1, keepdims=True) + EPS) * nw_ref[:, cols]
        part = jnp.dot(gnorm.astype(BF16), wout_ref[cols, :], preferred_element_type=F32)
        out = part if out is None else out + part
    o_ref[0] = x + _rmsnorm(out, wpost_ref[...])


def _ssd_layer(x, w_pre, w_post, w_in, conv_w, conv_b, dt_bias, a_log, d_skip, norm_w, w_out):
    bsz, s, d = x.shape
    d_inner = w_out.shape[0]
    nheads = dt_bias.shape[0]
    conv_dim = conv_w.shape[1]
    kc = conv_w.shape[0]
    ts = SSD_TS
    cblk = SSD_CONV_BLOCK
    grid = (bsz, s // ts)
    row = lambda v: v.reshape(1, -1)
    const2 = lambda shape: pl.BlockSpec(shape, lambda b, t: (0, 0), pipeline_mode=pl.Buffered(1))
    w_z = w_in[:, :d_inner].astype(BF16)
    w_xbc = w_in[:, d_inner:d_inner + conv_dim].astype(BF16)
    w_dt = w_in[:, d_inner + conv_dim:].astype(BF16)
    d_rep = jnp.repeat(d_skip, d_inner // nheads)
    in_specs = [
        pl.BlockSpec((1, ts, d), lambda b, t: (b, t, 0)),
        const2((1, d)),
        const2((d, d_inner)),
        const2((d, conv_dim)),
        const2((d, nheads)),
        const2((kc, conv_dim)),
        const2((1, conv_dim)),
        const2((1, nheads)),
        const2((1, nheads)),
        const2((1, d_inner)),
        const2((1, d_inner)),
        const2((d_inner, d)),
        const2((1, d)),
    ]
    return pl.pallas_call(
        _ssd_kernel,
        grid=grid,
        in_specs=in_specs,
        out_specs=pl.BlockSpec((1, ts, d), lambda b, t: (b, t, 0)),
        out_shape=jax.ShapeDtypeStruct(x.shape, x.dtype),
        scratch_shapes=[
            pltpu.VMEM((conv_dim // LANES, SUBLANES, LANES), F32),
            pltpu.VMEM((nheads // 2, SSD_D_STATE, 2 * SSD_HEADDIM), F32),
            pltpu.VMEM((2, cblk // LANES, SUBLANES + ts, LANES), F32),
            pltpu.VMEM((ts, conv_dim), F32),
        ],
        compiler_params=pltpu.CompilerParams(
            dimension_semantics=("arbitrary", "arbitrary"),
            vmem_limit_bytes=VMEM_LIMIT_BYTES),
        name="ssd_layer",
    )(x, row(w_pre), w_z, w_xbc, w_dt, conv_w, row(conv_b), row(dt_bias), row(a_log), row(d_rep),
      row(norm_w), w_out.astype(BF16), row(w_post))


def kernel(x, norm_mix_pre, norm_mix_post, norm_ffn_pre, norm_ffn_post, ssd_w_in, ssd_conv_w, ssd_conv_b, ssd_dt_bias, ssd_a_log, ssd_d, ssd_norm, ssd_w_out, lru_w_in, lru_b_in, lru_conv_w, lru_conv_b, lru_w_gx, lru_b_gx, lru_w_ga, lru_b_ga, lru_lambda, lru_w_out, lru_b_out, ffn_w_up, ffn_conv_w, ffn_conv_b, ffn_w_down):
    depth = norm_mix_pre.shape[0]
    for i in range(depth):
        j = i // 2
        if i % 2 == 0:
            x = _ssd_layer(x, norm_mix_pre[i], norm_mix_post[i], ssd_w_in[j], ssd_conv_w[j], ssd_conv_b[j],
                           ssd_dt_bias[j], ssd_a_log[j], ssd_d[j], ssd_norm[j], ssd_w_out[j])
        else:
            x = _lru_layer(x, norm_mix_pre[i], norm_mix_post[i], lru_w_in[j], lru_b_in[j], lru_conv_w[j],
                           lru_conv_b[j], lru_w_gx[j], lru_b_gx[j], lru_w_ga[j], lru_b_ga[j], lru_lambda[j],
                           lru_w_out[j], lru_b_out[j])
        x = _ffn_layer(x, norm_ffn_pre[i], norm_ffn_post[i], ffn_w_up[i], ffn_conv_w[i], ffn_conv_b[i],
                       ffn_w_down[i])
    return x
```

```python
import functools
import math

import jax
import jax.numpy as jnp
from jax import lax
from jax.experimental import pallas as pl
from jax.experimental.pallas import tpu as pltpu

F32 = jnp.float32
BF16 = jnp.bfloat16
EPS = 1e-6

SUBLANES = 8
LANES = 128
VMEM_LIMIT_BYTES = 56 * 1024 * 1024

SSD_HEADDIM = 64
SSD_GROUPS = 4
SSD_D_STATE = 128
SSD_CHUNK = 128
LRU_C = 8.0

FFN_TM = 512
FFN_FC = 1024
FFN_NSUB = 2
LRU_TL = 64
LRU_NSUB = 2
SSD_TS = 512
SSD_CONV_BLOCK = 1536


def _rmsnorm(x, w):
    ms = jnp.mean(x * x, axis=-1, keepdims=True)
    return x * lax.rsqrt(ms + EPS) * w


def _gelu_tanh(x):
    c = math.sqrt(2.0 / math.pi)
    return x * (0.5 * (1.0 + jnp.tanh(c * (x + 0.044715 * (x * x * x)))))


def _silu(x):
    hx = 0.5 * x
    return hx + hx * jnp.tanh(hx)


def _softplus(x):
    return jnp.maximum(x, 0.0) + jnp.log1p(jnp.exp(-jnp.abs(x)))


def _conv_via_slab(u, slab_ref, carry_ref, w, b, nseg=1):
    total, c = u.shape
    rows = total // nseg
    pitch = SUBLANES + rows
    k = w.shape[0]
    outs = []
    for j in range(c // LANES):
        ls = slice(j * LANES, (j + 1) * LANES)
        uj = u[:, ls]
        for s in range(nseg):
            seg = uj[s * rows:(s + 1) * rows]
            slab_ref[j, s * pitch:s * pitch + SUBLANES, :] = carry_ref[j, s * SUBLANES:(s + 1) * SUBLANES, :]
            slab_ref[j, s * pitch + SUBLANES:(s + 1) * pitch, :] = seg
            carry_ref[j, s * SUBLANES:(s + 1) * SUBLANES, :] = seg[rows - SUBLANES:]
        y = b[:, ls]
        for tap_i in range(k):
            shift = k - 1 - tap_i
            if shift == 0:
                tap = uj
            else:
                tap = jnp.concatenate(
                    [slab_ref[j, s * pitch + SUBLANES - shift:(s + 1) * pitch - shift, :] for s in range(nseg)],
                    axis=0)
            y = y + tap * w[tap_i:tap_i + 1, ls]
        outs.append(y)
    return jnp.concatenate(outs, axis=1)


def _ffn_kernel(x_ref, wpre_ref, wup_ref, cw_ref, cb_ref, wd_ref, wpost_ref, o_ref, cg_sc, cv_sc, slab_sc, *, nsub):
    t = pl.program_id(1)
    d_ff = wd_ref.shape[0]
    fc = cg_sc.shape[1] * LANES
    nf = d_ff // fc

    @pl.when(t == 0)
    def _():
        cg_sc[...] = jnp.zeros(cg_sc.shape, F32)
        cv_sc[...] = jnp.zeros(cv_sc.shape, F32)

    tm = x_ref.shape[1] // nsub
    hs = [_rmsnorm(x_ref[0, s * tm:(s + 1) * tm, :], wpre_ref[...]).astype(BF16) for s in range(nsub)]
    for s in range(nsub):
        h = hs[s]
        acc = None
        for f in range(nf):
            gs = slice(f * fc, (f + 1) * fc)
            vs = slice(d_ff + f * fc, d_ff + (f + 1) * fc)
            ug = jnp.dot(h, wup_ref[:, gs], preferred_element_type=F32)
            uv = jnp.dot(h, wup_ref[:, vs], preferred_element_type=F32)
            par = (s * nf + f) % 2
            g = _conv_via_slab(ug, slab_sc.at[par, 0], cg_sc.at[f], cw_ref[:, gs], cb_ref[:, gs])
            v = _conv_via_slab(uv, slab_sc.at[par, 1], cv_sc.at[f], cw_ref[:, vs], cb_ref[:, vs])
            a = (_gelu_tanh(g) * v).astype(BF16)
            part = jnp.dot(a, wd_ref[gs, :], preferred_element_type=F32)
            acc = part if acc is None else acc + part
        o_ref[0, s * tm:(s + 1) * tm, :] = x_ref[0, s * tm:(s + 1) * tm, :] + _rmsnorm(acc, wpost_ref[...])


def _stacked(shape, li, ngrid):
    zeros = (0,) * len(shape)
    if ngrid == 1:
        index_map = lambda t: (li,) + zeros
    else:
        index_map = lambda b, t: (li,) + zeros
    return pl.BlockSpec((None,) + tuple(shape), index_map, pipeline_mode=pl.Buffered(1))


def _rows(p):
    return p.reshape(p.shape[0], 1, p.shape[1])


def _ffn_layer(x, li, w_pre, w_post, w_up, conv_w, conv_b, w_down):
    bsz, s, d = x.shape
    d_ff = w_down.shape[1]
    kc = conv_w.shape[1]
    tm, fc, nsub = FFN_TM, FFN_FC, FFN_NSUB
    nf = d_ff // fc
    grid = (bsz, s // (tm * nsub))
    in_specs = [
        pl.BlockSpec((1, tm * nsub, d), lambda b, t: (b, t, 0)),
        _stacked((1, d), li, 2),
        _stacked((d, 2 * d_ff), li, 2),
        _stacked((kc, 2 * d_ff), li, 2),
        _stacked((1, 2 * d_ff), li, 2),
        _stacked((d_ff, d), li, 2),
        _stacked((1, d), li, 2),
    ]
    return pl.pallas_call(
        functools.partial(_ffn_kernel, nsub=nsub),
        grid=grid,
        in_specs=in_specs,
        out_specs=pl.BlockSpec((1, tm * nsub, d), lambda b, t: (b, t, 0)),
        out_shape=jax.ShapeDtypeStruct(x.shape, x.dtype),
        scratch_shapes=[
            pltpu.VMEM((nf, fc // LANES, SUBLANES, LANES), F32),
            pltpu.VMEM((nf, fc // LANES, SUBLANES, LANES), F32),
            pltpu.VMEM((2, 2, fc // LANES, SUBLANES + tm, LANES), F32),
        ],
        compiler_params=pltpu.CompilerParams(
            dimension_semantics=("arbitrary", "arbitrary"),
            vmem_limit_bytes=VMEM_LIMIT_BYTES),
        name="ffn_layer",
    )(x, _rows(w_pre), w_up, conv_w, _rows(conv_b), w_down, _rows(w_post))


def _lru_kernel(x_ref, wpre_ref, win_ref, bin_ref, cw_ref, cb_ref, wg_ref, bgx_ref, bga_ref, lam_ref,
                wout_ref, bout_ref, wpost_ref, o_ref, cx_sc, hc_sc, slab_sc, a_sc, b_sc):
    t = pl.program_id(0)
    nb, tl_all, d = x_ref.shape
    nsub = slab_sc.shape[0]
    tl = tl_all // nsub
    width = cw_ref.shape[1]
    nh = wg_ref.shape[0]
    blk = wg_ref.shape[1]
    nslab = width // LANES
    pitch = a_sc.shape[2] // nb

    @pl.when(t == 0)
    def _():
        cx_sc[...] = jnp.zeros(cx_sc.shape, F32)
        hc_sc[...] = jnp.zeros(hc_sc.shape, F32)

    k_lam = (0.5 * LRU_C) * _softplus(-lam_ref[...])
    half_bgx = 0.5 * bgx_ref[...]
    half_bga = 0.5 * bga_ref[...]
    hcur = [hc_sc[j] for j in range(nslab)]

    xs, projs = [], []
    for s in range(nsub):
        x = x_ref[:, s * tl:(s + 1) * tl, :].reshape(nb * tl, d)
        h = _rmsnorm(x, wpre_ref[...]).astype(BF16)
        xs.append(x)
        projs.append(jnp.dot(h, win_ref[...], preferred_element_type=F32) + bin_ref[...])

    for s in range(nsub):
        x, proj = xs[s], projs[s]
        ybr = _gelu_tanh(proj[:, :width])
        xbr = _conv_via_slab(proj[:, width:], slab_sc.at[s], cx_sc, cw_ref[...], cb_ref[...], nseg=nb)

        xb = xbr.astype(BF16)
        ux_parts, ua_parts = [], []
        for hd in range(nh):
            gg = jnp.dot(xb[:, hd * blk:(hd + 1) * blk], wg_ref[hd], preferred_element_type=F32)
            ux_parts.append(gg[:, :blk])
            ua_parts.append(gg[:, blk:])
        tx = jnp.tanh(0.5 * jnp.concatenate(ux_parts, axis=1) + half_bgx)
        ta = jnp.tanh(0.5 * jnp.concatenate(ua_parts, axis=1) + half_bga)
        neg_log_a = k_lam * ta + k_lam
        a_t = jnp.exp(-neg_log_a)
        mult = jnp.sqrt(jnp.tanh(neg_log_a) * (a_t * a_t + 1.0))
        b_t = (mult * xbr) * (0.5 * tx + 0.5)

        for j in range(nslab):
            ls = slice(j * LANES, (j + 1) * LANES)
            for bi in range(nb):
                a_sc[s, j, bi * pitch:bi * pitch + tl, :] = a_t[bi * tl:(bi + 1) * tl, ls]
                b_sc[s, j, bi * pitch:bi * pitch + tl, :] = b_t[bi * tl:(bi + 1) * tl, ls]
        for step in range(tl):
            for j in range(nslab):
                a_v = a_sc[s, j, pl.ds(step, nb, stride=pitch), :]
                b_v = b_sc[s, j, pl.ds(step, nb, stride=pitch), :]
                hcur[j] = a_v * hcur[j] + b_v
                a_sc[s, j, pl.ds(step, nb, stride=pitch), :] = hcur[j]
        hseq = jnp.concatenate(
            [jnp.concatenate([a_sc[s, j, bi * pitch:bi * pitch + tl, :] for bi in range(nb)], axis=0)
             for j in range(nslab)], axis=1)

        out = jnp.dot((hseq * ybr).astype(BF16), wout_ref[...], preferred_element_type=F32) + bout_ref[...]
        o_ref[:, s * tl:(s + 1) * tl, :] = (x + _rmsnorm(out, wpost_ref[...])).reshape(nb, tl, d)
    for j in range(nslab):
        hc_sc[j] = hcur[j]


def _lru_layer(x, ni, li, w_pre, w_post, w_in, b_in, conv_w, conv_b, w_g, b_gx, b_ga, lam, w_out, b_out):
    bsz, s, d = x.shape
    width = conv_w.shape[2]
    kc = conv_w.shape[1]
    nh, blk = w_g.shape[1], w_g.shape[2]
    tl, nsub = LRU_TL, LRU_NSUB
    nslab = width // LANES
    pitch = tl + SUBLANES
    grid = (s // (tl * nsub),)
    in_specs = [
        pl.BlockSpec((bsz, tl * nsub, d), lambda t: (0, t, 0)),
        _stacked((1, d), ni, 1),
        _stacked((d, 2 * width), li, 1),
        _stacked((1, 2 * width), li, 1),
        _stacked((kc, width), li, 1),
        _stacked((1, width), li, 1),
        _stacked((nh, blk, 2 * blk), li, 1),
        _stacked((1, width), li, 1),
        _stacked((1, width), li, 1),
        _stacked((1, width), li, 1),
        _stacked((width, d), li, 1),
        _stacked((1, d), li, 1),
        _stacked((1, d), ni, 1),
    ]
    return pl.pallas_call(
        _lru_kernel,
        grid=grid,
        in_specs=in_specs,
        out_specs=pl.BlockSpec((bsz, tl * nsub, d), lambda t: (0, t, 0)),
        out_shape=jax.ShapeDtypeStruct(x.shape, x.dtype),
        scratch_shapes=[
            pltpu.VMEM((nslab, bsz * SUBLANES, LANES), F32),
            pltpu.VMEM((nslab, bsz, LANES), F32),
            pltpu.VMEM((nsub, nslab, bsz * pitch, LANES), F32),
            pltpu.VMEM((nsub, nslab, bsz * pitch, LANES), F32),
            pltpu.VMEM((nsub, nslab, bsz * pitch, LANES), F32),
        ],
        compiler_params=pltpu.CompilerParams(
            dimension_semantics=("arbitrary",),
            vmem_limit_bytes=VMEM_LIMIT_BYTES),
        name="lru_layer",
    )(x, _rows(w_pre), w_in, _rows(b_in), conv_w, _rows(conv_b), w_g, _rows(b_gx), _rows(b_ga),
      _rows(lam), w_out, _rows(b_out), _rows(w_post))


def _ssd_kernel(x_ref, wpre_ref, win_ref, cw_ref, cb_ref, dtb_ref, alog_ref, drep_ref,
                nw_ref, wout_ref, wpost_ref, o_ref, cx_sc, st_sc, slab_sc, xbc_sc):
    t = pl.program_id(1)
    ts = x_ref.shape[1]
    L = SSD_CHUNK
    d_inner = wout_ref.shape[0]
    conv_dim = cw_ref.shape[1]
    n = SSD_D_STATE
    gn = SSD_GROUPS * n
    nheads = dtb_ref.shape[1]
    ppg = nheads // SSD_GROUPS // 2
    p = SSD_HEADDIM
    cblk = slab_sc.shape[1] * LANES
    nsl = cblk // LANES
    hp = jax.lax.Precision.HIGHEST
    log2e = math.log2(math.e)

    @pl.when(t == 0)
    def _():
        cx_sc[...] = jnp.zeros(cx_sc.shape, F32)
        st_sc[...] = jnp.zeros(st_sc.shape, F32)

    x = x_ref[0]
    h = _rmsnorm(x, wpre_ref[...]).astype(BF16)

    for j in range(conv_dim // cblk):
        cols = slice(j * cblk, (j + 1) * cblk)
        raw = jnp.dot(h, win_ref[:, d_inner + j * cblk:d_inner + (j + 1) * cblk], preferred_element_type=F32)
        xc = _conv_via_slab(raw, slab_sc.at[j % 2], cx_sc.at[j * nsl:(j + 1) * nsl], cw_ref[:, cols],
                            cb_ref[:, cols])
        xbc_sc[:, cols] = _silu(xc)

    dt_all = _softplus(jnp.dot(h, win_ref[:, d_inner + conv_dim:], preferred_element_type=F32)
                       + dtb_ref[...])
    da_all = dt_all * (-jnp.exp(alog_ref[...]))

    ri = lax.broadcasted_iota(jnp.int32, (L, L), 0)
    ci = lax.broadcasted_iota(jnp.int32, (L, L), 1)
    tri = ri >= ci
    tril = tri.astype(F32)
    triu = (ri <= ci).astype(F32)
    lo = lax.broadcasted_iota(jnp.int32, (1, 2 * p), 1) < p

    nch = ts // L
    da_cat = jnp.concatenate([da_all[c * L:(c + 1) * L] for c in range(nch)], axis=1)
    dt_cat = jnp.concatenate([dt_all[c * L:(c + 1) * L] for c in range(nch)], axis=1)
    cs_cat = jnp.dot(tril, da_cat, precision=hp, preferred_element_type=F32) * log2e
    cs_t_cat = lax.dot_general(da_cat, triu, (((0,), (0,)), ((), ())), precision=hp,
                               preferred_element_type=F32) * log2e
    dt_t_cat = dt_cat.T
    src_t_cat = cs_t_cat - jnp.log2(dt_t_cat)
    w_t_cat = dt_t_cat * jnp.exp2(cs_t_cat[:, L - 1:L] - cs_t_cat)
    et_cat = jnp.exp2(cs_cat[L - 1:L, :])

    gw = d_inner // SSD_GROUPS
    zw = d_inner // nch
    z_parts, y_chunks = [], []
    for c in range(nch):
        rows = slice(c * L, (c + 1) * L)
        z_parts.append(jnp.dot(h, win_ref[:, c * zw:(c + 1) * zw], preferred_element_type=F32))
        hs = slice(c * nheads, (c + 1) * nheads)
        cs = cs_cat[:, hs]
        src_t = src_t_cat[hs]
        w_t = w_t_cat[hs]
        et = et_cat[:, hs]
        y_parts = []
        for g in range(SSD_GROUPS):
            bg = xbc_sc[rows, d_inner + g * n:d_inner + (g + 1) * n]
            cg = xbc_sc[rows, d_inner + gn + g * n:d_inner + gn + (g + 1) * n]
            cb = lax.dot_general(cg.astype(BF16), bg.astype(BF16), (((1,), (1,)), ((), ())),
                                 preferred_element_type=F32)
            bg_t = bg.T
            for q in range(ppg):
                pair = g * ppg + q
                xs_q = xbc_sc[rows, pair * 2 * p:(pair + 1) * 2 * p]
                st_q = st_sc[pair]
                rhs = jnp.concatenate([xs_q, st_q], axis=0).astype(BF16)
                ys, us = [], []
                for e in range(2):
                    hd = 2 * pair + e
                    colb = jnp.broadcast_to(cs[:, hd:hd + 1], (L, L))
                    m = cb * jnp.exp2(jnp.where(tri, colb - src_t[hd:hd + 1, :], -jnp.inf))
                    ce = cg * jnp.exp2(colb)
                    lhs = jnp.concatenate([m, ce], axis=1).astype(BF16)
                    ys.append(jnp.dot(lhs, rhs, preferred_element_type=F32))
                    bw = (bg_t * w_t[hd:hd + 1, :]).astype(BF16)
                    us.append(jnp.dot(bw, rhs[:L], preferred_element_type=F32))
                y_parts.append(jnp.where(lo, ys[0], ys[1]))
                et_q = jnp.where(lo, et[:, 2 * pair:2 * pair + 1], et[:, 2 * pair + 1:2 * pair + 2])
                st_sc[pair] = st_q * et_q + jnp.where(lo, us[0], us[1])
        y_chunks.append(jnp.concatenate(y_parts, axis=1))

    z = jnp.concatenate(z_parts, axis=1)
    out = None
    for g in range(SSD_GROUPS):
        cols = slice(g * gw, (g + 1) * gw)
        y = jnp.concatenate([yc[:, cols] for yc in y_chunks], axis=0) + drep_ref[:, cols] * xbc_sc[:, cols]
        gz = y * _silu(z[:, cols])
        gnorm = gz * lax.rsqrt(jnp.mean(gz * gz, axis=-1, keepdims=True) + EPS) * nw_ref[:, cols]
        part = jnp.dot(gnorm.astype(BF16), wout_ref[cols, :], preferred_element_type=F32)
        out = part if out is None else out + part
    o_ref[0] = x + _rmsnorm(out, wpost_ref[...])


def _ssd_layer(x, ni, li, w_pre, w_post, w_in, conv_w, conv_b, dt_bias, a_log, d_rep, norm_w, w_out):
    bsz, s, d = x.shape
    d_inner = w_out.shape[1]
    nheads = dt_bias.shape[1]
    conv_dim = conv_w.shape[2]
    kc = conv_w.shape[1]
    ts = SSD_TS
    cblk = SSD_CONV_BLOCK
    grid = (bsz, s // ts)
    in_specs = [
        pl.BlockSpec((1, ts, d), lambda b, t: (b, t, 0)),
        _stacked((1, d), ni, 2),
        _stacked((d, w_in.shape[2]), li, 2),
        _stacked((kc, conv_dim), li, 2),
        _stacked((1, conv_dim), li, 2),
        _stacked((1, nheads), li, 2),
        _stacked((1, nheads), li, 2),
        _stacked((1, d_inner), li, 2),
        _stacked((1, d_inner), li, 2),
        _stacked((d_inner, d), li, 2),
        _stacked((1, d), ni, 2),
    ]
    return pl.pallas_call(
        _ssd_kernel,
        grid=grid,
        in_specs=in_specs,
        out_specs=pl.BlockSpec((1, ts, d), lambda b, t: (b, t, 0)),
        out_shape=jax.ShapeDtypeStruct(x.shape, x.dtype),
        scratch_shapes=[
            pltpu.VMEM((conv_dim // LANES, SUBLANES, LANES), F32),
            pltpu.VMEM((nheads // 2, SSD_D_STATE, 2 * SSD_HEADDIM), F32),
            pltpu.VMEM((2, cblk // LANES, SUBLANES + ts, LANES), F32),
            pltpu.VMEM((ts, conv_dim), F32),
        ],
        compiler_params=pltpu.CompilerParams(
            dimension_semantics=("arbitrary", "arbitrary"),
            vmem_limit_bytes=VMEM_LIMIT_BYTES),
        name="ssd_layer",
    )(x, _rows(w_pre), w_in, conv_w, _rows(conv_b), _rows(dt_bias), _rows(a_log), _rows(d_rep),
      _rows(norm_w), w_out, _rows(w_post))


def kernel(x, norm_mix_pre, norm_mix_post, norm_ffn_pre, norm_ffn_post, ssd_w_in, ssd_conv_w, ssd_conv_b, ssd_dt_bias, ssd_a_log, ssd_d, ssd_norm, ssd_w_out, lru_w_in, lru_b_in, lru_conv_w, lru_conv_b, lru_w_gx, lru_b_gx, lru_w_ga, lru_b_ga, lru_lambda, lru_w_out, lru_b_out, ffn_w_up, ffn_conv_w, ffn_conv_b, ffn_w_down):
    depth = norm_mix_pre.shape[0]
    ssd_w_in_b = ssd_w_in.astype(BF16)
    ssd_w_out_b = ssd_w_out.astype(BF16)
    ssd_d_rep = jnp.repeat(ssd_d, ssd_w_out.shape[1] // ssd_d.shape[1], axis=1)
    lru_w_in_b = lru_w_in.astype(BF16)
    lru_w_g_b = jnp.concatenate([lru_w_gx, lru_w_ga], axis=-1).astype(BF16)
    lru_w_out_b = lru_w_out.astype(BF16)
    ffn_w_up_b = ffn_w_up.astype(BF16)
    ffn_w_down_b = ffn_w_down.astype(BF16)
    for i in range(depth):
        j = i // 2
        if i % 2 == 0:
            x = _ssd_layer(x, i, j, norm_mix_pre, norm_mix_post, ssd_w_in_b, ssd_conv_w, ssd_conv_b,
                           ssd_dt_bias, ssd_a_log, ssd_d_rep, ssd_norm, ssd_w_out_b)
        else:
            x = _lru_layer(x, i, j, norm_mix_pre, norm_mix_post, lru_w_in_b, lru_b_in, lru_conv_w,
                           lru_conv_b, lru_w_g_b, lru_b_gx, lru_b_ga, lru_lambda, lru_w_out_b, lru_b_out)
        x = _ffn_layer(x, i, norm_ffn_pre, norm_ffn_post, ffn_w_up_b, ffn_conv_w, ffn_conv_b, ffn_w_down_b)
    return x
```

```python
import functools
import math

import jax
import jax.numpy as jnp
from jax import lax
from jax.experimental import pallas as pl
from jax.experimental.pallas import tpu as pltpu

F32 = jnp.float32
BF16 = jnp.bfloat16
EPS = 1e-6

SUBLANES = 8
LANES = 128
VMEM_LIMIT_BYTES = 56 * 1024 * 1024

SSD_HEADDIM = 64
SSD_GROUPS = 4
SSD_D_STATE = 128
SSD_CHUNK = 128
LRU_C = 8.0

FFN_TM = 512
FFN_FC = 1024
FFN_NSUB = 1
LRU_TL = 64
LRU_NSUB = 2
SSD_TS = 512
SSD_CONV_BLOCK = 1536


def _rmsnorm(x, w):
    ms = jnp.mean(x * x, axis=-1, keepdims=True)
    return x * lax.rsqrt(ms + EPS) * w


def _gelu_tanh(x):
    c = math.sqrt(2.0 / math.pi)
    return x * (0.5 * (1.0 + jnp.tanh(c * (x + 0.044715 * (x * x * x)))))


def _silu(x):
    hx = 0.5 * x
    return hx + hx * jnp.tanh(hx)


def _softplus(x):
    return jnp.maximum(x, 0.0) + jnp.log1p(jnp.exp(-jnp.abs(x)))


def _conv_via_slab(u, slab_ref, carry_ref, w, b, nseg=1):
    total, c = u.shape
    rows = total // nseg
    pitch = SUBLANES + rows
    k = w.shape[0]
    outs = []
    for j in range(c // LANES):
        ls = slice(j * LANES, (j + 1) * LANES)
        uj = u[:, ls]
        for s in range(nseg):
            seg = uj[s * rows:(s + 1) * rows]
            slab_ref[j, s * pitch:s * pitch + SUBLANES, :] = carry_ref[j, s * SUBLANES:(s + 1) * SUBLANES, :]
            slab_ref[j, s * pitch + SUBLANES:(s + 1) * pitch, :] = seg
            carry_ref[j, s * SUBLANES:(s + 1) * SUBLANES, :] = seg[rows - SUBLANES:]
        y = b[:, ls]
        for tap_i in range(k):
            shift = k - 1 - tap_i
            if shift == 0:
                tap = uj
            else:
                tap = jnp.concatenate(
                    [slab_ref[j, s * pitch + SUBLANES - shift:(s + 1) * pitch - shift, :] for s in range(nseg)],
                    axis=0)
            y = y + tap * w[tap_i:tap_i + 1, ls]
        outs.append(y)
    return jnp.concatenate(outs, axis=1)


def _ffn_kernel(x_ref, wpre_ref, wup_ref, cw_ref, cb_ref, wd_ref, wpost_ref, o_ref, cg_sc, cv_sc, slab_sc, *, nsub):
    t = pl.program_id(1)
    d_ff = wd_ref.shape[0]
    fc = cg_sc.shape[1] * LANES
    nf = d_ff // fc

    @pl.when(t == 0)
    def _():
        cg_sc[...] = jnp.zeros(cg_sc.shape, F32)
        cv_sc[...] = jnp.zeros(cv_sc.shape, F32)

    tm = x_ref.shape[1] // nsub
    hs = [_rmsnorm(x_ref[0, s * tm:(s + 1) * tm, :], wpre_ref[...]).astype(BF16) for s in range(nsub)]
    for s in range(nsub):
        h = hs[s]
        acc = None
        for f in range(nf):
            gs = slice(f * fc, (f + 1) * fc)
            vs = slice(d_ff + f * fc, d_ff + (f + 1) * fc)
            ug = jnp.dot(h, wup_ref[:, gs], preferred_element_type=F32)
            uv = jnp.dot(h, wup_ref[:, vs], preferred_element_type=F32)
            par = (s * nf + f) % 2
            g = _conv_via_slab(ug, slab_sc.at[par, 0], cg_sc.at[f], cw_ref[:, gs], cb_ref[:, gs])
            v = _conv_via_slab(uv, slab_sc.at[par, 1], cv_sc.at[f], cw_ref[:, vs], cb_ref[:, vs])
            a = (_gelu_tanh(g) * v).astype(BF16)
            part = jnp.dot(a, wd_ref[gs, :], preferred_element_type=F32)
            acc = part if acc is None else acc + part
        o_ref[0, s * tm:(s + 1) * tm, :] = x_ref[0, s * tm:(s + 1) * tm, :] + _rmsnorm(acc, wpost_ref[...])


def _stacked(shape, li, ngrid):
    zeros = (0,) * len(shape)
    if ngrid == 1:
        index_map = lambda t: (li,) + zeros
    else:
        index_map = lambda b, t: (li,) + zeros
    return pl.BlockSpec((None,) + tuple(shape), index_map, pipeline_mode=pl.Buffered(1))


def _rows(p):
    return p.reshape(p.shape[0], 1, p.shape[1])


def _ffn_layer(x, li, w_pre, w_post, w_up, conv_w, conv_b, w_down):
    bsz, s, d = x.shape
    d_ff = w_down.shape[1]
    kc = conv_w.shape[1]
    tm, fc, nsub = FFN_TM, FFN_FC, FFN_NSUB
    nf = d_ff // fc
    grid = (bsz, s // (tm * nsub))
    in_specs = [
        pl.BlockSpec((1, tm * nsub, d), lambda b, t: (b, t, 0)),
        _stacked((1, d), li, 2),
        _stacked((d, 2 * d_ff), li, 2),
        _stacked((kc, 2 * d_ff), li, 2),
        _stacked((1, 2 * d_ff), li, 2),
        _stacked((d_ff, d), li, 2),
        _stacked((1, d), li, 2),
    ]
    return pl.pallas_call(
        functools.partial(_ffn_kernel, nsub=nsub),
        grid=grid,
        in_specs=in_specs,
        out_specs=pl.BlockSpec((1, tm * nsub, d), lambda b, t: (b, t, 0)),
        out_shape=jax.ShapeDtypeStruct(x.shape, x.dtype),
        scratch_shapes=[
            pltpu.VMEM((nf, fc // LANES, SUBLANES, LANES), F32),
            pltpu.VMEM((nf, fc // LANES, SUBLANES, LANES), F32),
            pltpu.VMEM((2, 2, fc // LANES, SUBLANES + tm, LANES), F32),
        ],
        compiler_params=pltpu.CompilerParams(
            dimension_semantics=("arbitrary", "arbitrary"),
            vmem_limit_bytes=VMEM_LIMIT_BYTES),
        name="ffn_layer",
    )(x, _rows(w_pre), w_up, conv_w, _rows(conv_b), w_down, _rows(w_post))


def _lru_kernel(x_ref, wpre_ref, win_ref, bin_ref, cw_ref, cb_ref, wg_ref, bgx_ref, bga_ref, lam_ref,
                wout_ref, bout_ref, wpost_ref, o_ref, cx_sc, hc_sc, slab_sc, a_sc, b_sc):
    t = pl.program_id(0)
    nb, tl_all, d = x_ref.shape
    nsub = slab_sc.shape[0]
    tl = tl_all // nsub
    width = cw_ref.shape[1]
    nh = wg_ref.shape[0]
    blk = wg_ref.shape[1]
    nslab = width // LANES
    pitch = a_sc.shape[2] // nb

    @pl.when(t == 0)
    def _():
        cx_sc[...] = jnp.zeros(cx_sc.shape, F32)
        hc_sc[...] = jnp.zeros(hc_sc.shape, F32)

    k_lam = (0.5 * LRU_C) * _softplus(-lam_ref[...])
    half_bgx = 0.5 * bgx_ref[...]
    half_bga = 0.5 * bga_ref[...]
    hcur = [hc_sc[j] for j in range(nslab)]

    xs, projs = [], []
    for s in range(nsub):
        x = x_ref[:, s * tl:(s + 1) * tl, :].reshape(nb * tl, d)
        h = _rmsnorm(x, wpre_ref[...]).astype(BF16)
        xs.append(x)
        projs.append(jnp.dot(h, win_ref[...], preferred_element_type=F32) + bin_ref[...])

    for s in range(nsub):
        x, proj = xs[s], projs[s]
        ybr = _gelu_tanh(proj[:, :width])
        xbr = _conv_via_slab(proj[:, width:], slab_sc.at[s], cx_sc, cw_ref[...], cb_ref[...], nseg=nb)

        xb = xbr.astype(BF16)
        ux_parts, ua_parts = [], []
        for hd in range(nh):
            gg = jnp.dot(xb[:, hd * blk:(hd + 1) * blk], wg_ref[hd], preferred_element_type=F32)
            ux_parts.append(gg[:, :blk])
            ua_parts.append(gg[:, blk:])
        tx = jnp.tanh(0.5 * jnp.concatenate(ux_parts, axis=1) + half_bgx)
        ta = jnp.tanh(0.5 * jnp.concatenate(ua_parts, axis=1) + half_bga)
        neg_log_a = k_lam * ta + k_lam
        a_t = jnp.exp(-neg_log_a)
        m2 = jnp.tanh(neg_log_a) * (a_t * a_t + 1.0)
        mult = jnp.where(m2 > 0.0, m2 * lax.rsqrt(m2), 0.0)
        b_t = (mult * xbr) * (0.5 * tx + 0.5)

        for j in range(nslab):
            ls = slice(j * LANES, (j + 1) * LANES)
            for bi in range(nb):
                a_sc[s, j, bi * pitch:bi * pitch + tl, :] = a_t[bi * tl:(bi + 1) * tl, ls]
                b_sc[s, j, bi * pitch:bi * pitch + tl, :] = b_t[bi * tl:(bi + 1) * tl, ls]
        for step in range(tl):
            for j in range(nslab):
                a_v = a_sc[s, j, pl.ds(step, nb, stride=pitch), :]
                b_v = b_sc[s, j, pl.ds(step, nb, stride=pitch), :]
                hcur[j] = a_v * hcur[j] + b_v
                a_sc[s, j, pl.ds(step, nb, stride=pitch), :] = hcur[j]
        hseq = jnp.concatenate(
            [jnp.concatenate([a_sc[s, j, bi * pitch:bi * pitch + tl, :] for bi in range(nb)], axis=0)
             for j in range(nslab)], axis=1)

        out = jnp.dot((hseq * ybr).astype(BF16), wout_ref[...], preferred_element_type=F32) + bout_ref[...]
        o_ref[:, s * tl:(s + 1) * tl, :] = (x + _rmsnorm(out, wpost_ref[...])).reshape(nb, tl, d)
    for j in range(nslab):
        hc_sc[j] = hcur[j]


def _lru_layer(x, ni, li, w_pre, w_post, w_in, b_in, conv_w, conv_b, w_g, b_gx, b_ga, lam, w_out, b_out):
    bsz, s, d = x.shape
    width = conv_w.shape[2]
    kc = conv_w.shape[1]
    nh, blk = w_g.shape[1], w_g.shape[2]
    tl, nsub = LRU_TL, LRU_NSUB
    nslab = width // LANES
    pitch = tl + SUBLANES
    grid = (s // (tl * nsub),)
    in_specs = [
        pl.BlockSpec((bsz, tl * nsub, d), lambda t: (0, t, 0)),
        _stacked((1, d), ni, 1),
        _stacked((d, 2 * width), li, 1),
        _stacked((1, 2 * width), li, 1),
        _stacked((kc, width), li, 1),
        _stacked((1, width), li, 1),
        _stacked((nh, blk, 2 * blk), li, 1),
        _stacked((1, width), li, 1),
        _stacked((1, width), li, 1),
        _stacked((1, width), li, 1),
        _stacked((width, d), li, 1),
        _stacked((1, d), li, 1),
        _stacked((1, d), ni, 1),
    ]
    return pl.pallas_call(
        _lru_kernel,
        grid=grid,
        in_specs=in_specs,
        out_specs=pl.BlockSpec((bsz, tl * nsub, d), lambda t: (0, t, 0)),
        out_shape=jax.ShapeDtypeStruct(x.shape, x.dtype),
        scratch_shapes=[
            pltpu.VMEM((nslab, bsz * SUBLANES, LANES), F32),
            pltpu.VMEM((nslab, bsz, LANES), F32),
            pltpu.VMEM((nsub, nslab, bsz * pitch, LANES), F32),
            pltpu.VMEM((nsub, nslab, bsz * pitch, LANES), F32),
            pltpu.VMEM((nsub, nslab, bsz * pitch, LANES), F32),
        ],
        compiler_params=pltpu.CompilerParams(
            dimension_semantics=("arbitrary",),
            vmem_limit_bytes=VMEM_LIMIT_BYTES),
        name="lru_layer",
    )(x, _rows(w_pre), w_in, _rows(b_in), conv_w, _rows(conv_b), w_g, _rows(b_gx), _rows(b_ga),
      _rows(lam), w_out, _rows(b_out), _rows(w_post))


def _ssd_kernel(x_ref, wpre_ref, win_ref, cw_ref, cb_ref, dtb_ref, alog_ref, drep_ref,
                nw_ref, wout_ref, wpost_ref, o_ref, cx_sc, st_sc, slab_sc, xbc_sc):
    t = pl.program_id(1)
    ts = x_ref.shape[1]
    L = SSD_CHUNK
    d_inner = wout_ref.shape[0]
    conv_dim = cw_ref.shape[1]
    n = SSD_D_STATE
    gn = SSD_GROUPS * n
    nheads = dtb_ref.shape[1]
    ppg = nheads // SSD_GROUPS // 2
    p = SSD_HEADDIM
    cblk = slab_sc.shape[1] * LANES
    nsl = cblk // LANES
    hp = jax.lax.Precision.HIGHEST
    log2e = math.log2(math.e)

    @pl.when(t == 0)
    def _():
        cx_sc[...] = jnp.zeros(cx_sc.shape, F32)
        st_sc[...] = jnp.zeros(st_sc.shape, F32)

    x = x_ref[0]
    h = _rmsnorm(x, wpre_ref[...]).astype(BF16)

    for j in range(conv_dim // cblk):
        cols = slice(j * cblk, (j + 1) * cblk)
        raw = jnp.dot(h, win_ref[:, d_inner + j * cblk:d_inner + (j + 1) * cblk], preferred_element_type=F32)
        xc = _conv_via_slab(raw, slab_sc.at[j % 2], cx_sc.at[j * nsl:(j + 1) * nsl], cw_ref[:, cols],
                            cb_ref[:, cols])
        xbc_sc[:, cols] = _silu(xc)

    dt_all = _softplus(jnp.dot(h, win_ref[:, d_inner + conv_dim:], preferred_element_type=F32)
                       + dtb_ref[...])
    da_all = dt_all * (-jnp.exp(alog_ref[...]))

    ri = lax.broadcasted_iota(jnp.int32, (L, L), 0)
    ci = lax.broadcasted_iota(jnp.int32, (L, L), 1)
    tri = ri >= ci
    tril = tri.astype(F32)
    triu = (ri <= ci).astype(F32)
    lo = lax.broadcasted_iota(jnp.int32, (1, 2 * p), 1) < p

    nch = ts // L
    da_cat = jnp.concatenate([da_all[c * L:(c + 1) * L] for c in range(nch)], axis=1)
    dt_cat = jnp.concatenate([dt_all[c * L:(c + 1) * L] for c in range(nch)], axis=1)
    cs_cat = jnp.dot(tril, da_cat, precision=hp, preferred_element_type=F32) * log2e
    cs_t_cat = lax.dot_general(da_cat, triu, (((0,), (0,)), ((), ())), precision=hp,
                               preferred_element_type=F32) * log2e
    dt_t_cat = dt_cat.T
    src_t_cat = cs_t_cat - jnp.log2(dt_t_cat)
    w_t_cat = dt_t_cat * jnp.exp2(cs_t_cat[:, L - 1:L] - cs_t_cat)
    et_cat = jnp.exp2(cs_cat[L - 1:L, :])

    gw = d_inner // SSD_GROUPS
    zw = d_inner // nch
    z_parts, y_chunks = [], []
    for c in range(nch):
        rows = slice(c * L, (c + 1) * L)
        z_parts.append(jnp.dot(h, win_ref[:, c * zw:(c + 1) * zw], preferred_element_type=F32))
        hs = slice(c * nheads, (c + 1) * nheads)
        cs = cs_cat[:, hs]
        src_t = src_t_cat[hs]
        w_t = w_t_cat[hs]
        et = et_cat[:, hs]
        y_parts = []
        for g in range(SSD_GROUPS):
            bg = xbc_sc[rows, d_inner + g * n:d_inner + (g + 1) * n]
            cg = xbc_sc[rows, d_inner + gn + g * n:d_inner + gn + (g + 1) * n]
            cb = lax.dot_general(cg.astype(BF16), bg.astype(BF16), (((1,), (1,)), ((), ())),
                                 preferred_element_type=F32)
            bg_t = bg.T
            for q in range(ppg):
                pair = g * ppg + q
                xs_q = xbc_sc[rows, pair * 2 * p:(pair + 1) * 2 * p]
                st_q = st_sc[pair]
                rhs = jnp.concatenate([xs_q, st_q], axis=0).astype(BF16)
                ys, us = [], []
                for e in range(2):
                    hd = 2 * pair + e
                    colb = jnp.broadcast_to(cs[:, hd:hd + 1], (L, L))
                    m = cb * jnp.exp2(jnp.where(tri, colb - src_t[hd:hd + 1, :], -jnp.inf))
                    ce = cg * jnp.exp2(colb)
                    lhs = jnp.concatenate([m, ce], axis=1).astype(BF16)
                    ys.append(jnp.dot(lhs, rhs, preferred_element_type=F32))
                    bw = (bg_t * w_t[hd:hd + 1, :]).astype(BF16)
                    us.append(jnp.dot(bw, rhs[:L], preferred_element_type=F32))
                y_parts.append(jnp.where(lo, ys[0], ys[1]))
                et_q = jnp.where(lo, et[:, 2 * pair:2 * pair + 1], et[:, 2 * pair + 1:2 * pair + 2])
                st_sc[pair] = st_q * et_q + jnp.where(lo, us[0], us[1])
        y_chunks.append(jnp.concatenate(y_parts, axis=1))

    z = jnp.concatenate(z_parts, axis=1)
    out = None
    for g in range(SSD_GROUPS):
        cols = slice(g * gw, (g + 1) * gw)
        y = jnp.concatenate([yc[:, cols] for yc in y_chunks], axis=0) + drep_ref[:, cols] * xbc_sc[:, cols]
        gz = y * _silu(z[:, cols])
        gnorm = gz * lax.rsqrt(jnp.mean(gz * gz, axis=-1, keepdims=True) + EPS) * nw_ref[:, cols]
        part = jnp.dot(gnorm.astype(BF16), wout_ref[cols, :], preferred_element_type=F32)
        out = part if out is None else out + part
    o_ref[0] = x + _rmsnorm(out, wpost_ref[...])


def _ssd_layer(x, ni, li, w_pre, w_post, w_in, conv_w, conv_b, dt_bias, a_log, d_rep, norm_w, w_out):
    bsz, s, d = x.shape
    d_inner = w_out.shape[1]
    nheads = dt_bias.shape[1]
    conv_dim = conv_w.shape[2]
    kc = conv_w.shape[1]
    ts = SSD_TS
    cblk = SSD_CONV_BLOCK
    grid = (bsz, s // ts)
    in_specs = [
        pl.BlockSpec((1, ts, d), lambda b, t: (b, t, 0)),
        _stacked((1, d), ni, 2),
        _stacked((d, w_in.shape[2]), li, 2),
        _stacked((kc, conv_dim), li, 2),
        _stacked((1, conv_dim), li, 2),
        _stacked((1, nheads), li, 2),
        _stacked((1, nheads), li, 2),
        _stacked((1, d_inner), li, 2),
        _stacked((1, d_inner), li, 2),
        _stacked((d_inner, d), li, 2),
        _stacked((1, d), ni, 2),
    ]
    return pl.pallas_call(
        _ssd_kernel,
        grid=grid,
        in_specs=in_specs,
        out_specs=pl.BlockSpec((1, ts, d), lambda b, t: (b, t, 0)),
        out_shape=jax.ShapeDtypeStruct(x.shape, x.dtype),
        scratch_shapes=[
            pltpu.VMEM((conv_dim // LANES, SUBLANES, LANES), F32),
            pltpu.VMEM((nheads // 2, SSD_D_STATE, 2 * SSD_HEADDIM), F32),
            pltpu.VMEM((2, cblk // LANES, SUBLANES + ts, LANES), F32),
            pltpu.VMEM((ts, conv_dim), F32),
        ],
        compiler_params=pltpu.CompilerParams(
            dimension_semantics=("arbitrary", "arbitrary"),
            vmem_limit_bytes=VMEM_LIMIT_BYTES),
        name="ssd_layer",
    )(x, _rows(w_pre), w_in, conv_w, _rows(conv_b), _rows(dt_bias), _rows(a_log), _rows(d_rep),
      _rows(norm_w), w_out, _rows(w_post))


def kernel(x, norm_mix_pre, norm_mix_post, norm_ffn_pre, norm_ffn_post, ssd_w_in, ssd_conv_w, ssd_conv_b, ssd_dt_bias, ssd_a_log, ssd_d, ssd_norm, ssd_w_out, lru_w_in, lru_b_in, lru_conv_w, lru_conv_b, lru_w_gx, lru_b_gx, lru_w_ga, lru_b_ga, lru_lambda, lru_w_out, lru_b_out, ffn_w_up, ffn_conv_w, ffn_conv_b, ffn_w_down):
    depth = norm_mix_pre.shape[0]
    ssd_w_in_b = ssd_w_in.astype(BF16)
    ssd_w_out_b = ssd_w_out.astype(BF16)
    ssd_d_rep = jnp.repeat(ssd_d, ssd_w_out.shape[1] // ssd_d.shape[1], axis=1)
    lru_w_in_b = lru_w_in.astype(BF16)
    lru_w_g_b = jnp.concatenate([lru_w_gx, lru_w_ga], axis=-1).astype(BF16)
    lru_w_out_b = lru_w_out.astype(BF16)
    ffn_w_up_b = ffn_w_up.astype(BF16)
    ffn_w_down_b = ffn_w_down.astype(BF16)
    for i in range(depth):
        j = i // 2
        if i % 2 == 0:
            x = _ssd_layer(x, i, j, norm_mix_pre, norm_mix_post, ssd_w_in_b, ssd_conv_w, ssd_conv_b,
                           ssd_dt_bias, ssd_a_log, ssd_d_rep, ssd_norm, ssd_w_out_b)
        else:
            x = _lru_layer(x, i, j, norm_mix_pre, norm_mix_post, lru_w_in_b, lru_b_in, lru_conv_w,
                           lru_conv_b, lru_w_g_b, lru_b_gx, lru_b_ga, lru_lambda, lru_w_out_b, lru_b_out)
        x = _ffn_layer(x, i, norm_ffn_pre, norm_ffn_post, ffn_w_up_b, ffn_conv_w, ffn_conv_b, ffn_w_down_b)
    return x
```

```python
import functools
import math

import jax
import jax.numpy as jnp
from jax import lax
from jax.experimental import pallas as pl
from jax.experimental.pallas import tpu as pltpu

F32 = jnp.float32
BF16 = jnp.bfloat16
EPS = 1e-6

SUBLANES = 8
LANES = 128
VMEM_LIMIT_BYTES = 56 * 1024 * 1024

SSD_HEADDIM = 64
SSD_GROUPS = 4
SSD_D_STATE = 128
SSD_CHUNK = 128
LRU_C = 8.0

FFN_TM = 512
FFN_FC = 1024
FFN_NSUB = 1
LRU_TL = 64
LRU_NSUB = 2
SSD_TS = 512
SSD_CONV_BLOCK = 1536


def _rmsnorm(x, w):
    ms = jnp.mean(x * x, axis=-1, keepdims=True)
    return x * lax.rsqrt(ms + EPS) * w


def _gelu_tanh(x):
    c = math.sqrt(2.0 / math.pi)
    return x * (0.5 * (1.0 + jnp.tanh(c * (x + 0.044715 * (x * x * x)))))


def _silu(x):
    hx = 0.5 * x
    return hx + hx * jnp.tanh(hx)


def _softplus(x):
    return jnp.maximum(x, 0.0) + jnp.log1p(jnp.exp(-jnp.abs(x)))


def _conv_via_slab(u, slab_ref, carry_ref, w, b, nseg=1):
    total, c = u.shape
    rows = total // nseg
    pitch = SUBLANES + rows
    k = w.shape[0]
    outs = []
    for j in range(c // LANES):
        ls = slice(j * LANES, (j + 1) * LANES)
        uj = u[:, ls]
        for s in range(nseg):
            seg = uj[s * rows:(s + 1) * rows]
            slab_ref[j, s * pitch:s * pitch + SUBLANES, :] = carry_ref[j, s * SUBLANES:(s + 1) * SUBLANES, :]
            slab_ref[j, s * pitch + SUBLANES:(s + 1) * pitch, :] = seg
            carry_ref[j, s * SUBLANES:(s + 1) * SUBLANES, :] = seg[rows - SUBLANES:]
        y = b[:, ls]
        for tap_i in range(k):
            shift = k - 1 - tap_i
            if shift == 0:
                tap = uj
            else:
                tap = jnp.concatenate(
                    [slab_ref[j, s * pitch + SUBLANES - shift:(s + 1) * pitch - shift, :] for s in range(nseg)],
                    axis=0)
            y = y + tap * w[tap_i:tap_i + 1, ls]
        outs.append(y)
    return jnp.concatenate(outs, axis=1)


def _ffn_kernel(x_ref, wpre_ref, wup_ref, cw_ref, cb_ref, wd_ref, wpost_ref, o_ref, cg_sc, cv_sc, slab_sc, *, nsub):
    t = pl.program_id(1)
    d_ff = wd_ref.shape[0]
    fc = cg_sc.shape[1] * LANES
    nf = d_ff // fc

    @pl.when(t == 0)
    def _():
        cg_sc[...] = jnp.zeros(cg_sc.shape, F32)
        cv_sc[...] = jnp.zeros(cv_sc.shape, F32)

    tm = x_ref.shape[1] // nsub
    hs = [_rmsnorm(x_ref[0, s * tm:(s + 1) * tm, :], wpre_ref[...]).astype(BF16) for s in range(nsub)]
    for s in range(nsub):
        h = hs[s]
        acc = None
        for f in range(nf):
            gs = slice(f * fc, (f + 1) * fc)
            vs = slice(d_ff + f * fc, d_ff + (f + 1) * fc)
            ug = jnp.dot(h, wup_ref[:, gs], preferred_element_type=F32)
            uv = jnp.dot(h, wup_ref[:, vs], preferred_element_type=F32)
            par = (s * nf + f) % 2
            g = _conv_via_slab(ug, slab_sc.at[par, 0], cg_sc.at[f], cw_ref[:, gs], cb_ref[:, gs])
            v = _conv_via_slab(uv, slab_sc.at[par, 1], cv_sc.at[f], cw_ref[:, vs], cb_ref[:, vs])
            a = (_gelu_tanh(g) * v).astype(BF16)
            part = jnp.dot(a, wd_ref[gs, :], preferred_element_type=F32)
            acc = part if acc is None else acc + part
        o_ref[0, s * tm:(s + 1) * tm, :] = x_ref[0, s * tm:(s + 1) * tm, :] + _rmsnorm(acc, wpost_ref[...])


def _stacked(shape, li, ngrid):
    zeros = (0,) * len(shape)
    if ngrid == 1:
        index_map = lambda t: (li,) + zeros
    else:
        index_map = lambda b, t: (li,) + zeros
    return pl.BlockSpec((None,) + tuple(shape), index_map, pipeline_mode=pl.Buffered(1))


def _rows(p):
    return p.reshape(p.shape[0], 1, p.shape[1])


def _ffn_layer(x, li, w_pre, w_post, w_up, conv_w, conv_b, w_down):
    bsz, s, d = x.shape
    d_ff = w_down.shape[1]
    kc = conv_w.shape[1]
    tm, fc, nsub = FFN_TM, FFN_FC, FFN_NSUB
    nf = d_ff // fc
    grid = (bsz, s // (tm * nsub))
    in_specs = [
        pl.BlockSpec((1, tm * nsub, d), lambda b, t: (b, t, 0)),
        _stacked((1, d), li, 2),
        _stacked((d, 2 * d_ff), li, 2),
        _stacked((kc, 2 * d_ff), li, 2),
        _stacked((1, 2 * d_ff), li, 2),
        _stacked((d_ff, d), li, 2),
        _stacked((1, d), li, 2),
    ]
    return pl.pallas_call(
        functools.partial(_ffn_kernel, nsub=nsub),
        grid=grid,
        in_specs=in_specs,
        out_specs=pl.BlockSpec((1, tm * nsub, d), lambda b, t: (b, t, 0)),
        out_shape=jax.ShapeDtypeStruct(x.shape, x.dtype),
        scratch_shapes=[
            pltpu.VMEM((nf, fc // LANES, SUBLANES, LANES), F32),
            pltpu.VMEM((nf, fc // LANES, SUBLANES, LANES), F32),
            pltpu.VMEM((2, 2, fc // LANES, SUBLANES + tm, LANES), F32),
        ],
        compiler_params=pltpu.CompilerParams(
            dimension_semantics=("arbitrary", "arbitrary"),
            vmem_limit_bytes=VMEM_LIMIT_BYTES),
        name="ffn_layer",
    )(x, _rows(w_pre), w_up, conv_w, _rows(conv_b), w_down, _rows(w_post))


def _lru_kernel(x_ref, wpre_ref, win_ref, bin_ref, cw_ref, cb_ref, wg_ref, bgx_ref, bga_ref, lam_ref,
                wout_ref, bout_ref, wpost_ref, o_ref, cx_sc, hc_sc, slab_sc, a_sc, b_sc):
    t = pl.program_id(0)
    nb, tl_all, d = x_ref.shape
    nsub = slab_sc.shape[0]
    tl = tl_all // nsub
    width = cw_ref.shape[1]
    nh = wg_ref.shape[0]
    blk = wg_ref.shape[1]
    nslab = width // LANES
    pitch = a_sc.shape[2] // nb

    @pl.when(t == 0)
    def _():
        cx_sc[...] = jnp.zeros(cx_sc.shape, F32)
        hc_sc[...] = jnp.zeros(hc_sc.shape, F32)

    k_lam = (0.5 * LRU_C) * _softplus(-lam_ref[...])
    half_bgx = 0.5 * bgx_ref[...]
    half_bga = 0.5 * bga_ref[...]
    hcur = [hc_sc[j] for j in range(nslab)]

    xs, projs = [], []
    for s in range(nsub):
        x = x_ref[:, s * tl:(s + 1) * tl, :].reshape(nb * tl, d)
        h = _rmsnorm(x, wpre_ref[...]).astype(BF16)
        xs.append(x)
        projs.append(jnp.dot(h, win_ref[...], preferred_element_type=F32) + bin_ref[...])

    for s in range(nsub):
        x, proj = xs[s], projs[s]
        ybr = _gelu_tanh(proj[:, :width])
        xbr = _conv_via_slab(proj[:, width:], slab_sc.at[s], cx_sc, cw_ref[...], cb_ref[...], nseg=nb)

        xb = xbr.astype(BF16)
        ux_parts, ua_parts = [], []
        for hd in range(nh):
            gg = jnp.dot(xb[:, hd * blk:(hd + 1) * blk], wg_ref[hd], preferred_element_type=F32)
            ux_parts.append(gg[:, :blk])
            ua_parts.append(gg[:, blk:])
        tx = jnp.tanh(0.5 * jnp.concatenate(ux_parts, axis=1) + half_bgx)
        ta = jnp.tanh(0.5 * jnp.concatenate(ua_parts, axis=1) + half_bga)
        neg_log_a = k_lam * ta + k_lam
        a_t = jnp.exp(-neg_log_a)
        m2 = jnp.tanh(neg_log_a) * (a_t * a_t + 1.0)
        mult = jnp.where(m2 > 0.0, m2 * lax.rsqrt(m2), 0.0)
        b_t = (mult * xbr) * (0.5 * tx + 0.5)

        for j in range(nslab):
            ls = slice(j * LANES, (j + 1) * LANES)
            for bi in range(nb):
                a_sc[s, j, bi * pitch:bi * pitch + tl, :] = a_t[bi * tl:(bi + 1) * tl, ls]
                b_sc[s, j, bi * pitch:bi * pitch + tl, :] = b_t[bi * tl:(bi + 1) * tl, ls]
        for step in range(tl):
            for j in range(nslab):
                a_v = a_sc[s, j, pl.ds(step, nb, stride=pitch), :]
                b_v = b_sc[s, j, pl.ds(step, nb, stride=pitch), :]
                hcur[j] = a_v * hcur[j] + b_v
                a_sc[s, j, pl.ds(step, nb, stride=pitch), :] = hcur[j]
        hseq = jnp.concatenate(
            [jnp.concatenate([a_sc[s, j, bi * pitch:bi * pitch + tl, :] for bi in range(nb)], axis=0)
             for j in range(nslab)], axis=1)

        out = jnp.dot((hseq * ybr).astype(BF16), wout_ref[...], preferred_element_type=F32) + bout_ref[...]
        o_ref[:, s * tl:(s + 1) * tl, :] = (x + _rmsnorm(out, wpost_ref[...])).reshape(nb, tl, d)
    for j in range(nslab):
        hc_sc[j] = hcur[j]


def _lru_layer(x, ni, li, w_pre, w_post, w_in, b_in, conv_w, conv_b, w_g, b_gx, b_ga, lam, w_out, b_out):
    bsz, s, d = x.shape
    width = conv_w.shape[2]
    kc = conv_w.shape[1]
    nh, blk = w_g.shape[1], w_g.shape[2]
    tl, nsub = LRU_TL, LRU_NSUB
    nslab = width // LANES
    pitch = tl + SUBLANES
    grid = (s // (tl * nsub),)
    in_specs = [
        pl.BlockSpec((bsz, tl * nsub, d), lambda t: (0, t, 0)),
        _stacked((1, d), ni, 1),
        _stacked((d, 2 * width), li, 1),
        _stacked((1, 2 * width), li, 1),
        _stacked((kc, width), li, 1),
        _stacked((1, width), li, 1),
        _stacked((nh, blk, 2 * blk), li, 1),
        _stacked((1, width), li, 1),
        _stacked((1, width), li, 1),
        _stacked((1, width), li, 1),
        _stacked((width, d), li, 1),
        _stacked((1, d), li, 1),
        _stacked((1, d), ni, 1),
    ]
    return pl.pallas_call(
        _lru_kernel,
        grid=grid,
        in_specs=in_specs,
        out_specs=pl.BlockSpec((bsz, tl * nsub, d), lambda t: (0, t, 0)),
        out_shape=jax.ShapeDtypeStruct(x.shape, x.dtype),
        scratch_shapes=[
            pltpu.VMEM((nslab, bsz * SUBLANES, LANES), F32),
            pltpu.VMEM((nslab, bsz, LANES), F32),
            pltpu.VMEM((nsub, nslab, bsz * pitch, LANES), F32),
            pltpu.VMEM((nsub, nslab, bsz * pitch, LANES), F32),
            pltpu.VMEM((nsub, nslab, bsz * pitch, LANES), F32),
        ],
        compiler_params=pltpu.CompilerParams(
            dimension_semantics=("arbitrary",),
            vmem_limit_bytes=VMEM_LIMIT_BYTES),
        name="lru_layer",
    )(x, _rows(w_pre), w_in, _rows(b_in), conv_w, _rows(conv_b), w_g, _rows(b_gx), _rows(b_ga),
      _rows(lam), w_out, _rows(b_out), _rows(w_post))


def _ssd_kernel(x_ref, wpre_ref, win_ref, cw_ref, cb_ref, dtb_ref, alog_ref, drep_ref,
                nw_ref, wout_ref, wpost_ref, o_ref, cx_sc, st_sc, slab_sc, xbc_sc):
    t = pl.program_id(1)
    ts = x_ref.shape[1]
    L = SSD_CHUNK
    d_inner = wout_ref.shape[0]
    conv_dim = cw_ref.shape[1]
    n = SSD_D_STATE
    gn = SSD_GROUPS * n
    nheads = dtb_ref.shape[1]
    ppg = nheads // SSD_GROUPS // 2
    p = SSD_HEADDIM
    cblk = slab_sc.shape[1] * LANES
    nsl = cblk // LANES
    hp = jax.lax.Precision.HIGHEST
    log2e = math.log2(math.e)

    @pl.when(t == 0)
    def _():
        cx_sc[...] = jnp.zeros(cx_sc.shape, F32)
        st_sc[...] = jnp.zeros(st_sc.shape, F32)

    x = x_ref[0]
    h = _rmsnorm(x, wpre_ref[...]).astype(BF16)

    for j in range(conv_dim // cblk):
        cols = slice(j * cblk, (j + 1) * cblk)
        raw = jnp.dot(h, win_ref[:, d_inner + j * cblk:d_inner + (j + 1) * cblk], preferred_element_type=F32)
        xc = _conv_via_slab(raw, slab_sc.at[j % 2], cx_sc.at[j * nsl:(j + 1) * nsl], cw_ref[:, cols],
                            cb_ref[:, cols])
        xbc_sc[:, cols] = _silu(xc)

    dt_all = _softplus(jnp.dot(h, win_ref[:, d_inner + conv_dim:], preferred_element_type=F32)
                       + dtb_ref[...])
    da_all = dt_all * (-jnp.exp(alog_ref[...]))

    ri = lax.broadcasted_iota(jnp.int32, (L, L), 0)
    ci = lax.broadcasted_iota(jnp.int32, (L, L), 1)
    tri = ri >= ci
    tril = tri.astype(F32)
    triu = (ri <= ci).astype(F32)
    lo = lax.broadcasted_iota(jnp.int32, (1, 2 * p), 1) < p

    nch = ts // L
    da_cat = jnp.concatenate([da_all[c * L:(c + 1) * L] for c in range(nch)], axis=1)
    dt_cat = jnp.concatenate([dt_all[c * L:(c + 1) * L] for c in range(nch)], axis=1)
    cs_cat = jnp.dot(tril, da_cat, precision=hp, preferred_element_type=F32) * log2e
    cs_t_cat = lax.dot_general(da_cat, triu, (((0,), (0,)), ((), ())), precision=hp,
                               preferred_element_type=F32) * log2e
    dt_t_cat = dt_cat.T
    src_t_cat = cs_t_cat - jnp.log2(dt_t_cat)
    w_t_cat = dt_t_cat * jnp.exp2(cs_t_cat[:, L - 1:L] - cs_t_cat)
    et_cat = jnp.exp2(cs_cat[L - 1:L, :])

    gw = d_inner // SSD_GROUPS
    zw = d_inner // nch
    z_parts, y_chunks = [], []
    for c in range(nch):
        rows = slice(c * L, (c + 1) * L)
        z_parts.append(jnp.dot(h, win_ref[:, c * zw:(c + 1) * zw], preferred_element_type=F32))
        hs = slice(c * nheads, (c + 1) * nheads)
        cs = cs_cat[:, hs]
        src_t = src_t_cat[hs]
        w_t = w_t_cat[hs]
        et = et_cat[:, hs]
        y_parts = []
        for g in range(SSD_GROUPS):
            bg = xbc_sc[rows, d_inner + g * n:d_inner + (g + 1) * n]
            cg = xbc_sc[rows, d_inner + gn + g * n:d_inner + gn + (g + 1) * n]
            cb = lax.dot_general(cg.astype(BF16), bg.astype(BF16), (((1,), (1,)), ((), ())),
                                 preferred_element_type=F32)
            bg_t = bg.T
            cb_b = cb.astype(BF16)
            cg_b = cg.astype(BF16)
            for q in range(ppg):
                pair = g * ppg + q
                xs_q = xbc_sc[rows, pair * 2 * p:(pair + 1) * 2 * p]
                st_q = st_sc[pair]
                rhs = jnp.concatenate([xs_q, st_q], axis=0).astype(BF16)
                ys, us = [], []
                for e in range(2):
                    hd = 2 * pair + e
                    colb = jnp.broadcast_to(cs[:, hd:hd + 1], (L, L))
                    m = cb_b * jnp.exp2(jnp.where(tri, colb - src_t[hd:hd + 1, :], -jnp.inf)).astype(BF16)
                    ce = cg_b * jnp.exp2(colb).astype(BF16)
                    lhs = jnp.concatenate([m, ce], axis=1)
                    ys.append(jnp.dot(lhs, rhs, preferred_element_type=F32))
                    bw = (bg_t * w_t[hd:hd + 1, :]).astype(BF16)
                    us.append(jnp.dot(bw, rhs[:L], preferred_element_type=F32))
                y_parts.append(jnp.where(lo, ys[0], ys[1]))
                et_q = jnp.where(lo, et[:, 2 * pair:2 * pair + 1], et[:, 2 * pair + 1:2 * pair + 2])
                st_sc[pair] = st_q * et_q + jnp.where(lo, us[0], us[1])
        y_chunks.append(jnp.concatenate(y_parts, axis=1))

    z = jnp.concatenate(z_parts, axis=1)
    out = None
    for g in range(SSD_GROUPS):
        cols = slice(g * gw, (g + 1) * gw)
        y = jnp.concatenate([yc[:, cols] for yc in y_chunks], axis=0) + drep_ref[:, cols] * xbc_sc[:, cols]
        gz = y * _silu(z[:, cols])
        gnorm = gz * lax.rsqrt(jnp.mean(gz * gz, axis=-1, keepdims=True) + EPS) * nw_ref[:, cols]
        part = jnp.dot(gnorm.astype(BF16), wout_ref[cols, :], preferred_element_type=F32)
        out = part if out is None else out + part
    o_ref[0] = x + _rmsnorm(out, wpost_ref[...])


def _ssd_layer(x, ni, li, w_pre, w_post, w_in, conv_w, conv_b, dt_bias, a_log, d_rep, norm_w, w_out):
    bsz, s, d = x.shape
    d_inner = w_out.shape[1]
    nheads = dt_bias.shape[1]
    conv_dim = conv_w.shape[2]
    kc = conv_w.shape[1]
    ts = SSD_TS
    cblk = SSD_CONV_BLOCK
    grid = (bsz, s // ts)
    in_specs = [
        pl.BlockSpec((1, ts, d), lambda b, t: (b, t, 0)),
        _stacked((1, d), ni, 2),
        _stacked((d, w_in.shape[2]), li, 2),
        _stacked((kc, conv_dim), li, 2),
        _stacked((1, conv_dim), li, 2),
        _stacked((1, nheads), li, 2),
        _stacked((1, nheads), li, 2),
        _stacked((1, d_inner), li, 2),
        _stacked((1, d_inner), li, 2),
        _stacked((d_inner, d), li, 2),
        _stacked((1, d), ni, 2),
    ]
    return pl.pallas_call(
        _ssd_kernel,
        grid=grid,
        in_specs=in_specs,
        out_specs=pl.BlockSpec((1, ts, d), lambda b, t: (b, t, 0)),
        out_shape=jax.ShapeDtypeStruct(x.shape, x.dtype),
        scratch_shapes=[
            pltpu.VMEM((conv_dim // LANES, SUBLANES, LANES), F32),
            pltpu.VMEM((nheads // 2, SSD_D_STATE, 2 * SSD_HEADDIM), F32),
            pltpu.VMEM((2, cblk // LANES, SUBLANES + ts, LANES), F32),
            pltpu.VMEM((ts, conv_dim), F32),
        ],
        compiler_params=pltpu.CompilerParams(
            dimension_semantics=("arbitrary", "arbitrary"),
            vmem_limit_bytes=VMEM_LIMIT_BYTES),
        name="ssd_layer",
    )(x, _rows(w_pre), w_in, conv_w, _rows(conv_b), _rows(dt_bias), _rows(a_log), _rows(d_rep),
      _rows(norm_w), w_out, _rows(w_post))


def kernel(x, norm_mix_pre, norm_mix_post, norm_ffn_pre, norm_ffn_post, ssd_w_in, ssd_conv_w, ssd_conv_b, ssd_dt_bias, ssd_a_log, ssd_d, ssd_norm, ssd_w_out, lru_w_in, lru_b_in, lru_conv_w, lru_conv_b, lru_w_gx, lru_b_gx, lru_w_ga, lru_b_ga, lru_lambda, lru_w_out, lru_b_out, ffn_w_up, ffn_conv_w, ffn_conv_b, ffn_w_down):
    depth = norm_mix_pre.shape[0]
    ssd_w_in_b = ssd_w_in.astype(BF16)
    ssd_w_out_b = ssd_w_out.astype(BF16)
    ssd_d_rep = jnp.repeat(ssd_d, ssd_w_out.shape[1] // ssd_d.shape[1], axis=1)
    lru_w_in_b = lru_w_in.astype(BF16)
    lru_w_g_b = jnp.concatenate([lru_w_gx, lru_w_ga], axis=-1).astype(BF16)
    lru_w_out_b = lru_w_out.astype(BF16)
    ffn_w_up_b = ffn_w_up.astype(BF16)
    ffn_w_down_b = ffn_w_down.astype(BF16)
    for i in range(depth):
        j = i // 2
        if i % 2 == 0:
            x = _ssd_layer(x, i, j, norm_mix_pre, norm_mix_post, ssd_w_in_b, ssd_conv_w, ssd_conv_b,
                           ssd_dt_bias, ssd_a_log, ssd_d_rep, ssd_norm, ssd_w_out_b)
        else:
            x = _lru_layer(x, i, j, norm_mix_pre, norm_mix_post, lru_w_in_b, lru_b_in, lru_conv_w,
                           lru_conv_b, lru_w_g_b, lru_b_gx, lru_b_ga, lru_lambda, lru_w_out_b, lru_b_out)
        x = _ffn_layer(x, i, norm_ffn_pre, norm_ffn_post, ffn_w_up_b, ffn_conv_w, ffn_conv_b, ffn_w_down_b)
    return x
```

```python
import functools
import math

import jax
import jax.numpy as jnp
from jax import lax
from jax.experimental import pallas as pl
from jax.experimental.pallas import tpu as pltpu

F32 = jnp.float32
BF16 = jnp.bfloat16
EPS = 1e-6

SUBLANES = 8
LANES = 128
VMEM_LIMIT_BYTES = 56 * 1024 * 1024

SSD_HEADDIM = 64
SSD_GROUPS = 4
SSD_D_STATE = 128
SSD_CHUNK = 128
LRU_C = 8.0

FFN_TM = 512
FFN_FC = 1024
FFN_NSUB = 1
LRU_TL = 64
LRU_NSUB = 2
SSD_TS = 512
SSD_CONV_BLOCK = 1536


def _rmsnorm(x, w):
    ms = jnp.mean(x * x, axis=-1, keepdims=True)
    return x * lax.rsqrt(ms + EPS) * w


def _gelu_tanh(x):
    c = math.sqrt(2.0 / math.pi)
    return x * (0.5 * (1.0 + jnp.tanh(c * (x + 0.044715 * (x * x * x)))))


def _silu(x):
    hx = 0.5 * x
    return hx + hx * jnp.tanh(hx)


def _softplus(x):
    return jnp.maximum(x, 0.0) + jnp.log1p(jnp.exp(-jnp.abs(x)))


def _conv_via_slab(u, slab_ref, carry_ref, w, b, nseg=1):
    total, c = u.shape
    rows = total // nseg
    pitch = SUBLANES + rows
    k = w.shape[0]
    outs = []
    for j in range(c // LANES):
        ls = slice(j * LANES, (j + 1) * LANES)
        uj = u[:, ls]
        for s in range(nseg):
            seg = uj[s * rows:(s + 1) * rows]
            slab_ref[j, s * pitch:s * pitch + SUBLANES, :] = carry_ref[j, s * SUBLANES:(s + 1) * SUBLANES, :]
            slab_ref[j, s * pitch + SUBLANES:(s + 1) * pitch, :] = seg
            carry_ref[j, s * SUBLANES:(s + 1) * SUBLANES, :] = seg[rows - SUBLANES:]
        y = b[:, ls]
        for tap_i in range(k):
            shift = k - 1 - tap_i
            if shift == 0:
                tap = uj
            else:
                tap = jnp.concatenate(
                    [slab_ref[j, s * pitch + SUBLANES - shift:(s + 1) * pitch - shift, :] for s in range(nseg)],
                    axis=0)
            y = y + tap * w[tap_i:tap_i + 1, ls]
        outs.append(y)
    return jnp.concatenate(outs, axis=1)


def _ffn_kernel(x_ref, wpre_ref, wup_ref, cw_ref, cb_ref, wd_ref, wpost_ref, o_ref, cg_sc, cv_sc, slab_sc, *, nsub):
    t = pl.program_id(1)
    d_ff = wd_ref.shape[0]
    fc = cg_sc.shape[1] * LANES
    nf = d_ff // fc

    @pl.when(t == 0)
    def _():
        cg_sc[...] = jnp.zeros(cg_sc.shape, F32)
        cv_sc[...] = jnp.zeros(cv_sc.shape, F32)

    tm = x_ref.shape[1] // nsub
    hs = [_rmsnorm(x_ref[0, s * tm:(s + 1) * tm, :], wpre_ref[...]).astype(BF16) for s in range(nsub)]
    for s in range(nsub):
        h = hs[s]
        acc = None
        for f in range(nf):
            gs = slice(f * fc, (f + 1) * fc)
            vs = slice(d_ff + f * fc, d_ff + (f + 1) * fc)
            ug = jnp.dot(h, wup_ref[:, gs], preferred_element_type=F32)
            uv = jnp.dot(h, wup_ref[:, vs], preferred_element_type=F32)
            par = (s * nf + f) % 2
            g = _conv_via_slab(ug, slab_sc.at[par, 0], cg_sc.at[f], cw_ref[:, gs], cb_ref[:, gs])
            v = _conv_via_slab(uv, slab_sc.at[par, 1], cv_sc.at[f], cw_ref[:, vs], cb_ref[:, vs])
            a = (_gelu_tanh(g) * v).astype(BF16)
            part = jnp.dot(a, wd_ref[gs, :], preferred_element_type=F32)
            acc = part if acc is None else acc + part
        o_ref[0, s * tm:(s + 1) * tm, :] = x_ref[0, s * tm:(s + 1) * tm, :] + _rmsnorm(acc, wpost_ref[...])


def _stacked(shape, li, ngrid):
    zeros = (0,) * len(shape)
    if ngrid == 1:
        index_map = lambda t: (li,) + zeros
    else:
        index_map = lambda b, t: (li,) + zeros
    return pl.BlockSpec((None,) + tuple(shape), index_map, pipeline_mode=pl.Buffered(1))


def _rows(p):
    return p.reshape(p.shape[0], 1, p.shape[1])


def _ffn_layer(x, li, w_pre, w_post, w_up, conv_w, conv_b, w_down):
    bsz, s, d = x.shape
    d_ff = w_down.shape[1]
    kc = conv_w.shape[1]
    tm, fc, nsub = FFN_TM, FFN_FC, FFN_NSUB
    nf = d_ff // fc
    grid = (bsz, s // (tm * nsub))
    in_specs = [
        pl.BlockSpec((1, tm * nsub, d), lambda b, t: (b, t, 0)),
        _stacked((1, d), li, 2),
        _stacked((d, 2 * d_ff), li, 2),
        _stacked((kc, 2 * d_ff), li, 2),
        _stacked((1, 2 * d_ff), li, 2),
        _stacked((d_ff, d), li, 2),
        _stacked((1, d), li, 2),
    ]
    return pl.pallas_call(
        functools.partial(_ffn_kernel, nsub=nsub),
        grid=grid,
        in_specs=in_specs,
        out_specs=pl.BlockSpec((1, tm * nsub, d), lambda b, t: (b, t, 0)),
        out_shape=jax.ShapeDtypeStruct(x.shape, x.dtype),
        scratch_shapes=[
            pltpu.VMEM((nf, fc // LANES, SUBLANES, LANES), F32),
            pltpu.VMEM((nf, fc // LANES, SUBLANES, LANES), F32),
            pltpu.VMEM((2, 2, fc // LANES, SUBLANES + tm, LANES), F32),
        ],
        compiler_params=pltpu.CompilerParams(
            dimension_semantics=("arbitrary", "arbitrary"),
            vmem_limit_bytes=VMEM_LIMIT_BYTES),
        name="ffn_layer",
    )(x, _rows(w_pre), w_up, conv_w, _rows(conv_b), w_down, _rows(w_post))


def _lru_kernel(x_ref, wpre_ref, win_ref, bin_ref, cw_ref, cb_ref, wg_ref, bgx_ref, bga_ref, lam_ref,
                wout_ref, bout_ref, wpost_ref, o_ref, cx_sc, hc_sc, slab_sc, a_sc, b_sc):
    t = pl.program_id(0)
    nb, tl_all, d = x_ref.shape
    nsub = slab_sc.shape[0]
    tl = tl_all // nsub
    width = cw_ref.shape[1]
    nh = wg_ref.shape[0]
    blk = wg_ref.shape[1]
    nslab = width // LANES
    pitch = a_sc.shape[2] // nb

    @pl.when(t == 0)
    def _():
        cx_sc[...] = jnp.zeros(cx_sc.shape, F32)
        hc_sc[...] = jnp.zeros(hc_sc.shape, F32)

    k_lam = (0.5 * LRU_C) * _softplus(-lam_ref[...])
    half_bgx = 0.5 * bgx_ref[...]
    half_bga = 0.5 * bga_ref[...]
    hcur = [hc_sc[j] for j in range(nslab)]

    xs, projs = [], []
    for s in range(nsub):
        x = x_ref[:, s * tl:(s + 1) * tl, :].reshape(nb * tl, d)
        h = _rmsnorm(x, wpre_ref[...]).astype(BF16)
        xs.append(x)
        projs.append(jnp.dot(h, win_ref[...], preferred_element_type=F32) + bin_ref[...])

    for s in range(nsub):
        x, proj = xs[s], projs[s]
        ybr = _gelu_tanh(proj[:, :width])
        xbr = _conv_via_slab(proj[:, width:], slab_sc.at[s], cx_sc, cw_ref[...], cb_ref[...], nseg=nb)

        xb = xbr.astype(BF16)
        ux_parts, ua_parts = [], []
        for hd in range(nh):
            gg = jnp.dot(xb[:, hd * blk:(hd + 1) * blk], wg_ref[hd], preferred_element_type=F32)
            ux_parts.append(gg[:, :blk])
            ua_parts.append(gg[:, blk:])
        tx = jnp.tanh(0.5 * jnp.concatenate(ux_parts, axis=1) + half_bgx)
        ta = jnp.tanh(0.5 * jnp.concatenate(ua_parts, axis=1) + half_bga)
        neg_log_a = k_lam * ta + k_lam
        a_t = jnp.exp(-neg_log_a)
        m2 = jnp.tanh(neg_log_a) * (a_t * a_t + 1.0)
        mult = jnp.where(m2 > 0.0, m2 * lax.rsqrt(m2), 0.0)
        b_t = (mult * xbr) * (0.5 * tx + 0.5)

        for j in range(nslab):
            ls = slice(j * LANES, (j + 1) * LANES)
            for bi in range(nb):
                a_sc[s, j, bi * pitch:bi * pitch + tl, :] = a_t[bi * tl:(bi + 1) * tl, ls]
                b_sc[s, j, bi * pitch:bi * pitch + tl, :] = b_t[bi * tl:(bi + 1) * tl, ls]
        for step in range(tl):
            for j in range(nslab):
                a_v = a_sc[s, j, pl.ds(step, nb, stride=pitch), :]
                b_v = b_sc[s, j, pl.ds(step, nb, stride=pitch), :]
                hcur[j] = a_v * hcur[j] + b_v
                a_sc[s, j, pl.ds(step, nb, stride=pitch), :] = hcur[j]
        hseq = jnp.concatenate(
            [jnp.concatenate([a_sc[s, j, bi * pitch:bi * pitch + tl, :] for bi in range(nb)], axis=0)
             for j in range(nslab)], axis=1)

        out = jnp.dot(hseq.astype(BF16) * ybr.astype(BF16), wout_ref[...], preferred_element_type=F32) + bout_ref[...]
        o_ref[:, s * tl:(s + 1) * tl, :] = (x + _rmsnorm(out, wpost_ref[...])).reshape(nb, tl, d)
    for j in range(nslab):
        hc_sc[j] = hcur[j]


def _lru_layer(x, ni, li, w_pre, w_post, w_in, b_in, conv_w, conv_b, w_g, b_gx, b_ga, lam, w_out, b_out):
    bsz, s, d = x.shape
    width = conv_w.shape[2]
    kc = conv_w.shape[1]
    nh, blk = w_g.shape[1], w_g.shape[2]
    tl, nsub = LRU_TL, LRU_NSUB
    nslab = width // LANES
    pitch = tl + SUBLANES
    grid = (s // (tl * nsub),)
    in_specs = [
        pl.BlockSpec((bsz, tl * nsub, d), lambda t: (0, t, 0)),
        _stacked((1, d), ni, 1),
        _stacked((d, 2 * width), li, 1),
        _stacked((1, 2 * width), li, 1),
        _stacked((kc, width), li, 1),
        _stacked((1, width), li, 1),
        _stacked((nh, blk, 2 * blk), li, 1),
        _stacked((1, width), li, 1),
        _stacked((1, width), li, 1),
        _stacked((1, width), li, 1),
        _stacked((width, d), li, 1),
        _stacked((1, d), li, 1),
        _stacked((1, d), ni, 1),
    ]
    return pl.pallas_call(
        _lru_kernel,
        grid=grid,
        in_specs=in_specs,
        out_specs=pl.BlockSpec((bsz, tl * nsub, d), lambda t: (0, t, 0)),
        out_shape=jax.ShapeDtypeStruct(x.shape, x.dtype),
        scratch_shapes=[
            pltpu.VMEM((nslab, bsz * SUBLANES, LANES), F32),
            pltpu.VMEM((nslab, bsz, LANES), F32),
            pltpu.VMEM((nsub, nslab, bsz * pitch, LANES), F32),
            pltpu.VMEM((nsub, nslab, bsz * pitch, LANES), F32),
            pltpu.VMEM((nsub, nslab, bsz * pitch, LANES), F32),
        ],
        compiler_params=pltpu.CompilerParams(
            dimension_semantics=("arbitrary",),
            vmem_limit_bytes=VMEM_LIMIT_BYTES),
        name="lru_layer",
    )(x, _rows(w_pre), w_in, _rows(b_in), conv_w, _rows(conv_b), w_g, _rows(b_gx), _rows(b_ga),
      _rows(lam), w_out, _rows(b_out), _rows(w_post))


def _ssd_kernel(x_ref, wpre_ref, win_ref, cw_ref, cb_ref, dtb_ref, alog_ref, drep_ref,
                nw_ref, wout_ref, wpost_ref, o_ref, cx_sc, st_sc, slab_sc, xbc_sc):
    t = pl.program_id(1)
    ts = x_ref.shape[1]
    L = SSD_CHUNK
    d_inner = wout_ref.shape[0]
    conv_dim = cw_ref.shape[1]
    n = SSD_D_STATE
    gn = SSD_GROUPS * n
    nheads = dtb_ref.shape[1]
    ppg = nheads // SSD_GROUPS // 2
    p = SSD_HEADDIM
    cblk = slab_sc.shape[1] * LANES
    nsl = cblk // LANES
    hp = jax.lax.Precision.HIGHEST
    log2e = math.log2(math.e)

    @pl.when(t == 0)
    def _():
        cx_sc[...] = jnp.zeros(cx_sc.shape, F32)
        st_sc[...] = jnp.zeros(st_sc.shape, F32)

    x = x_ref[0]
    h = _rmsnorm(x, wpre_ref[...]).astype(BF16)

    for j in range(conv_dim // cblk):
        cols = slice(j * cblk, (j + 1) * cblk)
        raw = jnp.dot(h, win_ref[:, d_inner + j * cblk:d_inner + (j + 1) * cblk], preferred_element_type=F32)
        xc = _conv_via_slab(raw, slab_sc.at[j % 2], cx_sc.at[j * nsl:(j + 1) * nsl], cw_ref[:, cols],
                            cb_ref[:, cols])
        xbc_sc[:, cols] = _silu(xc)

    dt_all = _softplus(jnp.dot(h, win_ref[:, d_inner + conv_dim:], preferred_element_type=F32)
                       + dtb_ref[...])
    da_all = dt_all * (-jnp.exp(alog_ref[...]))

    ri = lax.broadcasted_iota(jnp.int32, (L, L), 0)
    ci = lax.broadcasted_iota(jnp.int32, (L, L), 1)
    tri = ri >= ci
    tril = tri.astype(F32)
    triu = (ri <= ci).astype(F32)
    lo = lax.broadcasted_iota(jnp.int32, (1, 2 * p), 1) < p

    nch = ts // L
    da_cat = jnp.concatenate([da_all[c * L:(c + 1) * L] for c in range(nch)], axis=1)
    dt_cat = jnp.concatenate([dt_all[c * L:(c + 1) * L] for c in range(nch)], axis=1)
    cs_cat = jnp.dot(tril, da_cat, precision=hp, preferred_element_type=F32) * log2e
    cs_t_cat = lax.dot_general(da_cat, triu, (((0,), (0,)), ((), ())), precision=hp,
                               preferred_element_type=F32) * log2e
    dt_t_cat = dt_cat.T
    src_t_cat = cs_t_cat - jnp.log2(dt_t_cat)
    w_t_cat = dt_t_cat * jnp.exp2(cs_t_cat[:, L - 1:L] - cs_t_cat)
    et_cat = jnp.exp2(cs_cat[L - 1:L, :])

    gw = d_inner // SSD_GROUPS
    zw = d_inner // nch
    z_parts, y_chunks = [], []
    for c in range(nch):
        rows = slice(c * L, (c + 1) * L)
        z_parts.append(jnp.dot(h, win_ref[:, c * zw:(c + 1) * zw], preferred_element_type=F32))
        hs = slice(c * nheads, (c + 1) * nheads)
        cs = cs_cat[:, hs]
        src_t = src_t_cat[hs]
        w_t = w_t_cat[hs]
        et = et_cat[:, hs]
        y_parts = []
        for g in range(SSD_GROUPS):
            bg = xbc_sc[rows, d_inner + g * n:d_inner + (g + 1) * n]
            cg = xbc_sc[rows, d_inner + gn + g * n:d_inner + gn + (g + 1) * n]
            cb = lax.dot_general(cg.astype(BF16), bg.astype(BF16), (((1,), (1,)), ((), ())),
                                 preferred_element_type=F32)
            bg_t = bg.T
            cb_b = cb.astype(BF16)
            cg_b = cg.astype(BF16)
            for q in range(ppg):
                pair = g * ppg + q
                xs_q = xbc_sc[rows, pair * 2 * p:(pair + 1) * 2 * p]
                st_q = st_sc[pair]
                rhs = jnp.concatenate([xs_q, st_q], axis=0).astype(BF16)
                ys, us = [], []
                for e in range(2):
                    hd = 2 * pair + e
                    colb = jnp.broadcast_to(cs[:, hd:hd + 1], (L, L))
                    m = cb_b * jnp.exp2(jnp.where(tri, colb - src_t[hd:hd + 1, :], -jnp.inf)).astype(BF16)
                    ce = cg_b * jnp.exp2(colb).astype(BF16)
                    lhs = jnp.concatenate([m, ce], axis=1)
                    ys.append(jnp.dot(lhs, rhs, preferred_element_type=F32))
                    bw = (bg_t * w_t[hd:hd + 1, :]).astype(BF16)
                    us.append(jnp.dot(bw, rhs[:L], preferred_element_type=F32))
                y_parts.append(jnp.where(lo, ys[0], ys[1]))
                et_q = jnp.where(lo, et[:, 2 * pair:2 * pair + 1], et[:, 2 * pair + 1:2 * pair + 2])
                st_sc[pair] = st_q * et_q + jnp.where(lo, us[0], us[1])
        y_chunks.append(jnp.concatenate(y_parts, axis=1))

    z = jnp.concatenate(z_parts, axis=1)
    out = None
    for g in range(SSD_GROUPS):
        cols = slice(g * gw, (g + 1) * gw)
        y = jnp.concatenate([yc[:, cols] for yc in y_chunks], axis=0) + drep_ref[:, cols] * xbc_sc[:, cols]
        gz = y * _silu(z[:, cols])
        gnorm = gz * lax.rsqrt(jnp.mean(gz * gz, axis=-1, keepdims=True) + EPS) * nw_ref[:, cols]
        part = jnp.dot(gnorm.astype(BF16), wout_ref[cols, :], preferred_element_type=F32)
        out = part if out is None else out + part
    o_ref[0] = x + _rmsnorm(out, wpost_ref[...])


def _ssd_layer(x, ni, li, w_pre, w_post, w_in, conv_w, conv_b, dt_bias, a_log, d_rep, norm_w, w_out):
    bsz, s, d = x.shape
    d_inner = w_out.shape[1]
    nheads = dt_bias.shape[1]
    conv_dim = conv_w.shape[2]
    kc = conv_w.shape[1]
    ts = SSD_TS
    cblk = SSD_CONV_BLOCK
    grid = (bsz, s // ts)
    in_specs = [
        pl.BlockSpec((1, ts, d), lambda b, t: (b, t, 0)),
        _stacked((1, d), ni, 2),
        _stacked((d, w_in.shape[2]), li, 2),
        _stacked((kc, conv_dim), li, 2),
        _stacked((1, conv_dim), li, 2),
        _stacked((1, nheads), li, 2),
        _stacked((1, nheads), li, 2),
        _stacked((1, d_inner), li, 2),
        _stacked((1, d_inner), li, 2),
        _stacked((d_inner, d), li, 2),
        _stacked((1, d), ni, 2),
    ]
    return pl.pallas_call(
        _ssd_kernel,
        grid=grid,
        in_specs=in_specs,
        out_specs=pl.BlockSpec((1, ts, d), lambda b, t: (b, t, 0)),
        out_shape=jax.ShapeDtypeStruct(x.shape, x.dtype),
        scratch_shapes=[
            pltpu.VMEM((conv_dim // LANES, SUBLANES, LANES), F32),
            pltpu.VMEM((nheads // 2, SSD_D_STATE, 2 * SSD_HEADDIM), F32),
            pltpu.VMEM((2, cblk // LANES, SUBLANES + ts, LANES), F32),
            pltpu.VMEM((ts, conv_dim), F32),
        ],
        compiler_params=pltpu.CompilerParams(
            dimension_semantics=("arbitrary", "arbitrary"),
            vmem_limit_bytes=VMEM_LIMIT_BYTES),
        name="ssd_layer",
    )(x, _rows(w_pre), w_in, conv_w, _rows(conv_b), _rows(dt_bias), _rows(a_log), _rows(d_rep),
      _rows(norm_w), w_out, _rows(w_post))


def kernel(x, norm_mix_pre, norm_mix_post, norm_ffn_pre, norm_ffn_post, ssd_w_in, ssd_conv_w, ssd_conv_b, ssd_dt_bias, ssd_a_log, ssd_d, ssd_norm, ssd_w_out, lru_w_in, lru_b_in, lru_conv_w, lru_conv_b, lru_w_gx, lru_b_gx, lru_w_ga, lru_b_ga, lru_lambda, lru_w_out, lru_b_out, ffn_w_up, ffn_conv_w, ffn_conv_b, ffn_w_down):
    depth = norm_mix_pre.shape[0]
    ssd_w_in_b = ssd_w_in.astype(BF16)
    ssd_w_out_b = ssd_w_out.astype(BF16)
    ssd_d_rep = jnp.repeat(ssd_d, ssd_w_out.shape[1] // ssd_d.shape[1], axis=1)
    lru_w_in_b = lru_w_in.astype(BF16)
    lru_w_g_b = jnp.concatenate([lru_w_gx, lru_w_ga], axis=-1).astype(BF16)
    lru_w_out_b = lru_w_out.astype(BF16)
    ffn_w_up_b = ffn_w_up.astype(BF16)
    ffn_w_down_b = ffn_w_down.astype(BF16)
    for i in range(depth):
        j = i // 2
        if i % 2 == 0:
            x = _ssd_layer(x, i, j, norm_mix_pre, norm_mix_post, ssd_w_in_b, ssd_conv_w, ssd_conv_b,
                           ssd_dt_bias, ssd_a_log, ssd_d_rep, ssd_norm, ssd_w_out_b)
        else:
            x = _lru_layer(x, i, j, norm_mix_pre, norm_mix_post, lru_w_in_b, lru_b_in, lru_conv_w,
                           lru_conv_b, lru_w_g_b, lru_b_gx, lru_b_ga, lru_lambda, lru_w_out_b, lru_b_out)
        x = _ffn_layer(x, i, norm_ffn_pre, norm_ffn_post, ffn_w_up_b, ffn_conv_w, ffn_conv_b, ffn_w_down_b)
    return x
```

```python
import functools
import math

import jax
import jax.numpy as jnp
from jax import lax
from jax.experimental import pallas as pl
from jax.experimental.pallas import tpu as pltpu

F32 = jnp.float32
BF16 = jnp.bfloat16
EPS = 1e-6

SUBLANES = 8
LANES = 128
VMEM_LIMIT_BYTES = 56 * 1024 * 1024

SSD_HEADDIM = 64
SSD_GROUPS = 4
SSD_D_STATE = 128
SSD_CHUNK = 128
LRU_C = 8.0

FFN_TM = 512
FFN_FC = 2048
FFN_NSUB = 1
LRU_TL = 64
LRU_NSUB = 2
SSD_TS = 512
SSD_CONV_BLOCK = 1536


def _rmsnorm(x, w):
    ms = jnp.mean(x * x, axis=-1, keepdims=True)
    return x * lax.rsqrt(ms + EPS) * w


def _gelu_tanh(x):
    c = math.sqrt(2.0 / math.pi)
    return x * (0.5 * (1.0 + jnp.tanh(c * (x + 0.044715 * (x * x * x)))))


def _silu(x):
    hx = 0.5 * x
    return hx + hx * jnp.tanh(hx)


def _softplus(x):
    return jnp.maximum(x, 0.0) + jnp.log1p(jnp.exp(-jnp.abs(x)))


def _conv_via_slab(u, slab_ref, carry_ref, w, b, nseg=1):
    total, c = u.shape
    rows = total // nseg
    pitch = SUBLANES + rows
    k = w.shape[0]
    outs = []
    for j in range(c // LANES):
        ls = slice(j * LANES, (j + 1) * LANES)
        uj = u[:, ls]
        for s in range(nseg):
            seg = uj[s * rows:(s + 1) * rows]
            slab_ref[j, s * pitch:s * pitch + SUBLANES, :] = carry_ref[j, s * SUBLANES:(s + 1) * SUBLANES, :]
            slab_ref[j, s * pitch + SUBLANES:(s + 1) * pitch, :] = seg
            carry_ref[j, s * SUBLANES:(s + 1) * SUBLANES, :] = seg[rows - SUBLANES:]
        y = b[:, ls]
        for tap_i in range(k):
            shift = k - 1 - tap_i
            if shift == 0:
                tap = uj
            else:
                tap = jnp.concatenate(
                    [slab_ref[j, s * pitch + SUBLANES - shift:(s + 1) * pitch - shift, :] for s in range(nseg)],
                    axis=0)
            y = y + tap * w[tap_i:tap_i + 1, ls]
        outs.append(y)
    return jnp.concatenate(outs, axis=1)


def _ffn_kernel(x_ref, wpre_ref, wup_ref, cw_ref, cb_ref, wd_ref, wpost_ref, o_ref, cg_sc, cv_sc, slab_sc, *, nsub):
    t = pl.program_id(1)
    d_ff = wd_ref.shape[0]
    fc = cg_sc.shape[1] * LANES
    nf = d_ff // fc

    @pl.when(t == 0)
    def _():
        cg_sc[...] = jnp.zeros(cg_sc.shape, F32)
        cv_sc[...] = jnp.zeros(cv_sc.shape, F32)

    tm = x_ref.shape[1] // nsub
    hs = [_rmsnorm(x_ref[0, s * tm:(s + 1) * tm, :], wpre_ref[...]).astype(BF16) for s in range(nsub)]
    for s in range(nsub):
        h = hs[s]
        acc = None
        for f in range(nf):
            gs = slice(f * fc, (f + 1) * fc)
            vs = slice(d_ff + f * fc, d_ff + (f + 1) * fc)
            ug = jnp.dot(h, wup_ref[:, gs], preferred_element_type=F32)
            uv = jnp.dot(h, wup_ref[:, vs], preferred_element_type=F32)
            par = (s * nf + f) % 2
            g = _conv_via_slab(ug, slab_sc.at[par, 0], cg_sc.at[f], cw_ref[:, gs], cb_ref[:, gs])
            v = _conv_via_slab(uv, slab_sc.at[par, 1], cv_sc.at[f], cw_ref[:, vs], cb_ref[:, vs])
            a = (_gelu_tanh(g) * v).astype(BF16)
            part = jnp.dot(a, wd_ref[gs, :], preferred_element_type=F32)
            acc = part if acc is None else acc + part
        o_ref[0, s * tm:(s + 1) * tm, :] = x_ref[0, s * tm:(s + 1) * tm, :] + _rmsnorm(acc, wpost_ref[...])


def _stacked(shape, li, ngrid):
    zeros = (0,) * len(shape)
    if ngrid == 1:
        index_map = lambda t: (li,) + zeros
    else:
        index_map = lambda b, t: (li,) + zeros
    return pl.BlockSpec((None,) + tuple(shape), index_map, pipeline_mode=pl.Buffered(1))


def _rows(p):
    return p.reshape(p.shape[0], 1, p.shape[1])


def _ffn_layer(x, li, w_pre, w_post, w_up, conv_w, conv_b, w_down):
    bsz, s, d = x.shape
    d_ff = w_down.shape[1]
    kc = conv_w.shape[1]
    tm, fc, nsub = FFN_TM, FFN_FC, FFN_NSUB
    nf = d_ff // fc
    grid = (bsz, s // (tm * nsub))
    in_specs = [
        pl.BlockSpec((1, tm * nsub, d), lambda b, t: (b, t, 0)),
        _stacked((1, d), li, 2),
        _stacked((d, 2 * d_ff), li, 2),
        _stacked((kc, 2 * d_ff), li, 2),
        _stacked((1, 2 * d_ff), li, 2),
        _stacked((d_ff, d), li, 2),
        _stacked((1, d), li, 2),
    ]
    return pl.pallas_call(
        functools.partial(_ffn_kernel, nsub=nsub),
        grid=grid,
        in_specs=in_specs,
        out_specs=pl.BlockSpec((1, tm * nsub, d), lambda b, t: (b, t, 0)),
        out_shape=jax.ShapeDtypeStruct(x.shape, x.dtype),
        scratch_shapes=[
            pltpu.VMEM((nf, fc // LANES, SUBLANES, LANES), F32),
            pltpu.VMEM((nf, fc // LANES, SUBLANES, LANES), F32),
            pltpu.VMEM((2, 2, fc // LANES, SUBLANES + tm, LANES), F32),
        ],
        compiler_params=pltpu.CompilerParams(
            dimension_semantics=("arbitrary", "arbitrary"),
            vmem_limit_bytes=VMEM_LIMIT_BYTES),
        name="ffn_layer",
    )(x, _rows(w_pre), w_up, conv_w, _rows(conv_b), w_down, _rows(w_post))


def _lru_kernel(x_ref, wpre_ref, win_ref, bin_ref, cw_ref, cb_ref, wg_ref, bgx_ref, bga_ref, lam_ref,
                wout_ref, bout_ref, wpost_ref, o_ref, cx_sc, hc_sc, slab_sc, a_sc, b_sc):
    t = pl.program_id(0)
    nb, tl_all, d = x_ref.shape
    nsub = slab_sc.shape[0]
    tl = tl_all // nsub
    width = cw_ref.shape[1]
    nh = wg_ref.shape[0]
    blk = wg_ref.shape[1]
    nslab = width // LANES
    pitch = a_sc.shape[2] // nb

    @pl.when(t == 0)
    def _():
        cx_sc[...] = jnp.zeros(cx_sc.shape, F32)
        hc_sc[...] = jnp.zeros(hc_sc.shape, F32)

    k_lam = (0.5 * LRU_C) * _softplus(-lam_ref[...])
    half_bgx = 0.5 * bgx_ref[...]
    half_bga = 0.5 * bga_ref[...]
    hcur = [hc_sc[j] for j in range(nslab)]

    xs, projs = [], []
    for s in range(nsub):
        x = x_ref[:, s * tl:(s + 1) * tl, :].reshape(nb * tl, d)
        h = _rmsnorm(x, wpre_ref[...]).astype(BF16)
        xs.append(x)
        projs.append(jnp.dot(h, win_ref[...], preferred_element_type=F32) + bin_ref[...])

    for s in range(nsub):
        x, proj = xs[s], projs[s]
        ybr = _gelu_tanh(proj[:, :width])
        xbr = _conv_via_slab(proj[:, width:], slab_sc.at[s], cx_sc, cw_ref[...], cb_ref[...], nseg=nb)

        xb = xbr.astype(BF16)
        ux_parts, ua_parts = [], []
        for hd in range(nh):
            gg = jnp.dot(xb[:, hd * blk:(hd + 1) * blk], wg_ref[hd], preferred_element_type=F32)
            ux_parts.append(gg[:, :blk])
            ua_parts.append(gg[:, blk:])
        tx = jnp.tanh(0.5 * jnp.concatenate(ux_parts, axis=1) + half_bgx)
        ta = jnp.tanh(0.5 * jnp.concatenate(ua_parts, axis=1) + half_bga)
        neg_log_a = k_lam * ta + k_lam
        a_t = jnp.exp(-neg_log_a)
        m2 = jnp.tanh(neg_log_a) * (a_t * a_t + 1.0)
        mult = jnp.where(m2 > 0.0, m2 * lax.rsqrt(m2), 0.0)
        b_t = (mult * xbr) * (0.5 * tx + 0.5)

        for j in range(nslab):
            ls = slice(j * LANES, (j + 1) * LANES)
            for bi in range(nb):
                a_sc[s, j, bi * pitch:bi * pitch + tl, :] = a_t[bi * tl:(bi + 1) * tl, ls]
                b_sc[s, j, bi * pitch:bi * pitch + tl, :] = b_t[bi * tl:(bi + 1) * tl, ls]
        for step in range(tl):
            for j in range(nslab):
                a_v = a_sc[s, j, pl.ds(step, nb, stride=pitch), :]
                b_v = b_sc[s, j, pl.ds(step, nb, stride=pitch), :]
                hcur[j] = a_v * hcur[j] + b_v
                a_sc[s, j, pl.ds(step, nb, stride=pitch), :] = hcur[j]
        hseq = jnp.concatenate(
            [jnp.concatenate([a_sc[s, j, bi * pitch:bi * pitch + tl, :] for bi in range(nb)], axis=0)
             for j in range(nslab)], axis=1)

        out = jnp.dot(hseq.astype(BF16) * ybr.astype(BF16), wout_ref[...], preferred_element_type=F32) + bout_ref[...]
        o_ref[:, s * tl:(s + 1) * tl, :] = (x + _rmsnorm(out, wpost_ref[...])).reshape(nb, tl, d)
    for j in range(nslab):
        hc_sc[j] = hcur[j]


def _lru_layer(x, ni, li, w_pre, w_post, w_in, b_in, conv_w, conv_b, w_g, b_gx, b_ga, lam, w_out, b_out):
    bsz, s, d = x.shape
    width = conv_w.shape[2]
    kc = conv_w.shape[1]
    nh, blk = w_g.shape[1], w_g.shape[2]
    tl, nsub = LRU_TL, LRU_NSUB
    nslab = width // LANES
    pitch = tl + SUBLANES
    grid = (s // (tl * nsub),)
    in_specs = [
        pl.BlockSpec((bsz, tl * nsub, d), lambda t: (0, t, 0)),
        _stacked((1, d), ni, 1),
        _stacked((d, 2 * width), li, 1),
        _stacked((1, 2 * width), li, 1),
        _stacked((kc, width), li, 1),
        _stacked((1, width), li, 1),
        _stacked((nh, blk, 2 * blk), li, 1),
        _stacked((1, width), li, 1),
        _stacked((1, width), li, 1),
        _stacked((1, width), li, 1),
        _stacked((width, d), li, 1),
        _stacked((1, d), li, 1),
        _stacked((1, d), ni, 1),
    ]
    return pl.pallas_call(
        _lru_kernel,
        grid=grid,
        in_specs=in_specs,
        out_specs=pl.BlockSpec((bsz, tl * nsub, d), lambda t: (0, t, 0)),
        out_shape=jax.ShapeDtypeStruct(x.shape, x.dtype),
        scratch_shapes=[
            pltpu.VMEM((nslab, bsz * SUBLANES, LANES), F32),
            pltpu.VMEM((nslab, bsz, LANES), F32),
            pltpu.VMEM((nsub, nslab, bsz * pitch, LANES), F32),
            pltpu.VMEM((nsub, nslab, bsz * pitch, LANES), F32),
            pltpu.VMEM((nsub, nslab, bsz * pitch, LANES), F32),
        ],
        compiler_params=pltpu.CompilerParams(
            dimension_semantics=("arbitrary",),
            vmem_limit_bytes=VMEM_LIMIT_BYTES),
        name="lru_layer",
    )(x, _rows(w_pre), w_in, _rows(b_in), conv_w, _rows(conv_b), w_g, _rows(b_gx), _rows(b_ga),
      _rows(lam), w_out, _rows(b_out), _rows(w_post))


def _ssd_kernel(x_ref, wpre_ref, win_ref, cw_ref, cb_ref, dtb_ref, alog_ref, drep_ref,
                nw_ref, wout_ref, wpost_ref, o_ref, cx_sc, st_sc, slab_sc, xbc_sc):
    t = pl.program_id(1)
    ts = x_ref.shape[1]
    L = SSD_CHUNK
    d_inner = wout_ref.shape[0]
    conv_dim = cw_ref.shape[1]
    n = SSD_D_STATE
    gn = SSD_GROUPS * n
    nheads = dtb_ref.shape[1]
    ppg = nheads // SSD_GROUPS // 2
    p = SSD_HEADDIM
    cblk = slab_sc.shape[1] * LANES
    nsl = cblk // LANES
    hp = jax.lax.Precision.HIGHEST
    log2e = math.log2(math.e)

    @pl.when(t == 0)
    def _():
        cx_sc[...] = jnp.zeros(cx_sc.shape, F32)
        st_sc[...] = jnp.zeros(st_sc.shape, F32)

    x = x_ref[0]
    h = _rmsnorm(x, wpre_ref[...]).astype(BF16)

    for j in range(conv_dim // cblk):
        cols = slice(j * cblk, (j + 1) * cblk)
        raw = jnp.dot(h, win_ref[:, d_inner + j * cblk:d_inner + (j + 1) * cblk], preferred_element_type=F32)
        xc = _conv_via_slab(raw, slab_sc.at[j % 2], cx_sc.at[j * nsl:(j + 1) * nsl], cw_ref[:, cols],
                            cb_ref[:, cols])
        xbc_sc[:, cols] = _silu(xc)

    dt_all = _softplus(jnp.dot(h, win_ref[:, d_inner + conv_dim:], preferred_element_type=F32)
                       + dtb_ref[...])
    da_all = dt_all * (-jnp.exp(alog_ref[...]))

    ri = lax.broadcasted_iota(jnp.int32, (L, L), 0)
    ci = lax.broadcasted_iota(jnp.int32, (L, L), 1)
    tri = ri >= ci
    tril = tri.astype(F32)
    triu = (ri <= ci).astype(F32)
    lo = lax.broadcasted_iota(jnp.int32, (1, 2 * p), 1) < p

    nch = ts // L
    da_cat = jnp.concatenate([da_all[c * L:(c + 1) * L] for c in range(nch)], axis=1)
    dt_cat = jnp.concatenate([dt_all[c * L:(c + 1) * L] for c in range(nch)], axis=1)
    cs_cat = jnp.dot(tril, da_cat, precision=hp, preferred_element_type=F32) * log2e
    cs_t_cat = lax.dot_general(da_cat, triu, (((0,), (0,)), ((), ())), precision=hp,
                               preferred_element_type=F32) * log2e
    dt_t_cat = dt_cat.T
    src_t_cat = cs_t_cat - jnp.log2(dt_t_cat)
    w_t_cat = dt_t_cat * jnp.exp2(cs_t_cat[:, L - 1:L] - cs_t_cat)
    et_cat = jnp.exp2(cs_cat[L - 1:L, :])

    gw = d_inner // SSD_GROUPS
    zw = d_inner // nch
    z_parts, y_chunks = [], []
    for c in range(nch):
        rows = slice(c * L, (c + 1) * L)
        z_parts.append(jnp.dot(h, win_ref[:, c * zw:(c + 1) * zw], preferred_element_type=F32))
        hs = slice(c * nheads, (c + 1) * nheads)
        cs = cs_cat[:, hs]
        src_t = src_t_cat[hs]
        w_t = w_t_cat[hs]
        et = et_cat[:, hs]
        y_parts = []
        for g in range(SSD_GROUPS):
            bg = xbc_sc[rows, d_inner + g * n:d_inner + (g + 1) * n]
            cg = xbc_sc[rows, d_inner + gn + g * n:d_inner + gn + (g + 1) * n]
            cb = lax.dot_general(cg.astype(BF16), bg.astype(BF16), (((1,), (1,)), ((), ())),
                                 preferred_element_type=F32)
            bg_t = bg.T
            cb_b = cb.astype(BF16)
            cg_b = cg.astype(BF16)
            for q in range(ppg):
                pair = g * ppg + q
                xs_q = xbc_sc[rows, pair * 2 * p:(pair + 1) * 2 * p]
                st_q = st_sc[pair]
                rhs = jnp.concatenate([xs_q, st_q], axis=0).astype(BF16)
                ys, us = [], []
                for e in range(2):
                    hd = 2 * pair + e
                    colb = jnp.broadcast_to(cs[:, hd:hd + 1], (L, L))
                    m = cb_b * jnp.exp2(jnp.where(tri, colb - src_t[hd:hd + 1, :], -jnp.inf)).astype(BF16)
                    ce = cg_b * jnp.exp2(colb).astype(BF16)
                    lhs = jnp.concatenate([m, ce], axis=1)
                    ys.append(jnp.dot(lhs, rhs, preferred_element_type=F32))
                    bw = (bg_t * w_t[hd:hd + 1, :]).astype(BF16)
                    us.append(jnp.dot(bw, rhs[:L], preferred_element_type=F32))
                y_parts.append(jnp.where(lo, ys[0], ys[1]))
                et_q = jnp.where(lo, et[:, 2 * pair:2 * pair + 1], et[:, 2 * pair + 1:2 * pair + 2])
                st_sc[pair] = st_q * et_q + jnp.where(lo, us[0], us[1])
        y_chunks.append(jnp.concatenate(y_parts, axis=1))

    z = jnp.concatenate(z_parts, axis=1)
    out = None
    for g in range(SSD_GROUPS):
        cols = slice(g * gw, (g + 1) * gw)
        y = jnp.concatenate([yc[:, cols] for yc in y_chunks], axis=0) + drep_ref[:, cols] * xbc_sc[:, cols]
        gz = y * _silu(z[:, cols])
        gnorm = gz * lax.rsqrt(jnp.mean(gz * gz, axis=-1, keepdims=True) + EPS) * nw_ref[:, cols]
        part = jnp.dot(gnorm.astype(BF16), wout_ref[cols, :], preferred_element_type=F32)
        out = part if out is None else out + part
    o_ref[0] = x + _rmsnorm(out, wpost_ref[...])


def _ssd_layer(x, ni, li, w_pre, w_post, w_in, conv_w, conv_b, dt_bias, a_log, d_rep, norm_w, w_out):
    bsz, s, d = x.shape
    d_inner = w_out.shape[1]
    nheads = dt_bias.shape[1]
    conv_dim = conv_w.shape[2]
    kc = conv_w.shape[1]
    ts = SSD_TS
    cblk = SSD_CONV_BLOCK
    grid = (bsz, s // ts)
    in_specs = [
        pl.BlockSpec((1, ts, d), lambda b, t: (b, t, 0)),
        _stacked((1, d), ni, 2),
        _stacked((d, w_in.shape[2]), li, 2),
        _stacked((kc, conv_dim), li, 2),
        _stacked((1, conv_dim), li, 2),
        _stacked((1, nheads), li, 2),
        _stacked((1, nheads), li, 2),
        _stacked((1, d_inner), li, 2),
        _stacked((1, d_inner), li, 2),
        _stacked((d_inner, d), li, 2),
        _stacked((1, d), ni, 2),
    ]
    return pl.pallas_call(
        _ssd_kernel,
        grid=grid,
        in_specs=in_specs,
        out_specs=pl.BlockSpec((1, ts, d), lambda b, t: (b, t, 0)),
        out_shape=jax.ShapeDtypeStruct(x.shape, x.dtype),
        scratch_shapes=[
            pltpu.VMEM((conv_dim // LANES, SUBLANES, LANES), F32),
            pltpu.VMEM((nheads // 2, SSD_D_STATE, 2 * SSD_HEADDIM), F32),
            pltpu.VMEM((2, cblk // LANES, SUBLANES + ts, LANES), F32),
            pltpu.VMEM((ts, conv_dim), F32),
        ],
        compiler_params=pltpu.CompilerParams(
            dimension_semantics=("arbitrary", "arbitrary"),
            vmem_limit_bytes=VMEM_LIMIT_BYTES),
        name="ssd_layer",
    )(x, _rows(w_pre), w_in, conv_w, _rows(conv_b), _rows(dt_bias), _rows(a_log), _rows(d_rep),
      _rows(norm_w), w_out, _rows(w_post))


def kernel(x, norm_mix_pre, norm_mix_post, norm_ffn_pre, norm_ffn_post, ssd_w_in, ssd_conv_w, ssd_conv_b, ssd_dt_bias, ssd_a_log, ssd_d, ssd_norm, ssd_w_out, lru_w_in, lru_b_in, lru_conv_w, lru_conv_b, lru_w_gx, lru_b_gx, lru_w_ga, lru_b_ga, lru_lambda, lru_w_out, lru_b_out, ffn_w_up, ffn_conv_w, ffn_conv_b, ffn_w_down):
    depth = norm_mix_pre.shape[0]
    ssd_w_in_b = ssd_w_in.astype(BF16)
    ssd_w_out_b = ssd_w_out.astype(BF16)
    ssd_d_rep = jnp.repeat(ssd_d, ssd_w_out.shape[1] // ssd_d.shape[1], axis=1)
    lru_w_in_b = lru_w_in.astype(BF16)
    lru_w_g_b = jnp.concatenate([lru_w_gx, lru_w_ga], axis=-1).astype(BF16)
    lru_w_out_b = lru_w_out.astype(BF16)
    ffn_w_up_b = ffn_w_up.astype(BF16)
    ffn_w_down_b = ffn_w_down.astype(BF16)
    for i in range(depth):
        j = i // 2
        if i % 2 == 0:
            x = _ssd_layer(x, i, j, norm_mix_pre, norm_mix_post, ssd_w_in_b, ssd_conv_w, ssd_conv_b,
                           ssd_dt_bias, ssd_a_log, ssd_d_rep, ssd_norm, ssd_w_out_b)
        else:
            x = _lru_layer(x, i, j, norm_mix_pre, norm_mix_post, lru_w_in_b, lru_b_in, lru_conv_w,
                           lru_conv_b, lru_w_g_b, lru_b_gx, lru_b_ga, lru_lambda, lru_w_out_b, lru_b_out)
        x = _ffn_layer(x, i, norm_ffn_pre, norm_ffn_post, ffn_w_up_b, ffn_conv_w, ffn_conv_b, ffn_w_down_b)
    return x
```
